```python
import jax, jax.numpy as jnp
from jax import lax
import numpy as np

D_MODEL = 1024
BATCH = 8
SEQ = 2048
DEPTH = 1

CHUNK = 64
D_MIX = D_MODEL
D_LRU = D_MIX // 2
LRU_BLOCKS = 8
LRU_BLOCK_DIM = D_LRU // LRU_BLOCKS
CONV_WIDTH = 4
LRU_C = 8.0
N_SB_HEADS = 8
SB_HEAD_DIM = 64
D_SB = N_SB_HEADS * SB_HEAD_DIM
Q_BLOCK = 128
D_IN_PROJ = 2 * D_LRU + 3 * D_SB
N_EXPERTS = 32
TOP_K = 4
D_EXPERT = D_MODEL
SWIGLU_LIMIT = 7.0
SWIGLU_ALPHA = 1.702
MOE_ROW_BLOCK = 256
D_PLE = 256
RMS_EPS = 1e-6

kernel_name = "hymba_rglru_stickbreak_moe_ple"


def rms_norm(x, g):
    xf = x.astype(jnp.float32)
    y = xf * lax.rsqrt(jnp.mean(xf * xf, axis=-1, keepdims=True) + RMS_EPS)
    return (y * g.astype(jnp.float32)).astype(x.dtype)


def causal_depthwise_conv(x, w, b):
    out = lax.conv_general_dilated(
        x, w[:, None, :], window_strides=(1,), padding=[(CONV_WIDTH - 1, 0)],
        dimension_numbers=("NWC", "WIO", "NWC"), feature_group_count=x.shape[-1])
    return out + b


def rg_lru(x, w_a, b_a, w_x, b_x, lam):
    B, S, C = x.shape
    xf = x.astype(jnp.float32)
    xb = xf.reshape(B, S, LRU_BLOCKS, LRU_BLOCK_DIM)
    r = jax.nn.sigmoid(jnp.einsum("bsni,nij->bsnj", xb, w_a.astype(jnp.float32)).reshape(B, S, C) + b_a)
    i = jax.nn.sigmoid(jnp.einsum("bsni,nij->bsnj", xb, w_x.astype(jnp.float32)).reshape(B, S, C) + b_x)
    log_a = -LRU_C * r * jax.nn.softplus(-lam.astype(jnp.float32))
    a = jnp.exp(log_a)
    b = jnp.sqrt(-jnp.expm1(2.0 * log_a)) * (i * xf)

    def combine(lhs, rhs):
        a1, b1 = lhs
        a2, b2 = rhs
        return a1 * a2, a2 * b1 + b2

    _, h = lax.associative_scan(combine, (a, b), axis=1)
    return h.astype(x.dtype)


def stick_breaking_attention(q, k, v):
    B, S, H, Dh = q.shape
    scale = SB_HEAD_DIM ** -0.5
    qf = q.astype(jnp.float32).transpose(0, 2, 1, 3)
    kf = k.astype(jnp.float32).transpose(0, 2, 1, 3)
    vf = v.astype(jnp.float32).transpose(0, 2, 1, 3)
    outs = []
    for blk in range(S // Q_BLOCK):
        q0 = blk * Q_BLOCK
        kv_len = q0 + Q_BLOCK
        z = jnp.einsum("bhqd,bhkd->bhqk", qf[:, :, q0:kv_len], kf[:, :, :kv_len]) * scale
        q_pos = q0 + jnp.arange(Q_BLOCK)
        k_pos = jnp.arange(kv_len)
        visible = k_pos[None, :] < q_pos[:, None]
        log_keep = jnp.where(visible, jax.nn.log_sigmoid(-z), 0.0)
        between = lax.cumsum(log_keep, axis=3, reverse=True) - log_keep
        weights = jnp.where(visible, jnp.exp(jax.nn.log_sigmoid(z) + between), 0.0)
        outs.append(jnp.einsum("bhqk,bhkd->bhqd", weights, vf[:, :, :kv_len]))
    o = jnp.concatenate(outs, axis=2)
    return o.transpose(0, 2, 1, 3).reshape(B, S, H * Dh).astype(q.dtype)


def moe_ffn(x, w_router, b_router, w_up, b_up, w_down, b_down):
    B, S, D = x.shape
    T = B * S
    xt = x.reshape(T, D)
    logits = (xt @ w_router + b_router).astype(jnp.float32)
    top_vals, top_idx = lax.top_k(logits, TOP_K)
    gates = jax.nn.softmax(top_vals, axis=-1).astype(x.dtype)

    n_assign = T * TOP_K
    expert_flat = top_idx.reshape(n_assign)
    token_flat = jnp.arange(n_assign, dtype=jnp.int32) // TOP_K
    gate_flat = gates.reshape(n_assign)
    order = jnp.argsort(expert_flat)
    sorted_expert = expert_flat[order]
    counts = jnp.zeros((N_EXPERTS,), jnp.int32).at[expert_flat].add(1)
    start = jnp.cumsum(counts) - counts
    padded = (counts + MOE_ROW_BLOCK - 1) // MOE_ROW_BLOCK * MOE_ROW_BLOCK
    padded_end = jnp.cumsum(padded)
    padded_start = padded_end - padded
    dest = padded_start[sorted_expert] + (jnp.arange(n_assign, dtype=jnp.int32) - start[sorted_expert])
    n_blocks = -(-n_assign // MOE_ROW_BLOCK) + N_EXPERTS
    n_rows = n_blocks * MOE_ROW_BLOCK
    row_token = jnp.zeros((n_rows,), jnp.int32).at[dest].set(token_flat[order])
    row_gate = jnp.zeros((n_rows,), x.dtype).at[dest].set(gate_flat[order])
    block_expert = jnp.clip(
        jnp.searchsorted(padded_end, jnp.arange(n_blocks, dtype=jnp.int32) * MOE_ROW_BLOCK, side="right"),
        0, N_EXPERTS - 1)

    def expert_block(args):
        e, tok = args
        xb = xt[tok]
        hdn = xb @ w_up[e] + b_up[e]
        g = jnp.minimum(hdn[:, :D_EXPERT], SWIGLU_LIMIT)
        u = jnp.clip(hdn[:, D_EXPERT:], -SWIGLU_LIMIT, SWIGLU_LIMIT)
        glu = g * jax.nn.sigmoid(SWIGLU_ALPHA * g)
        return ((u + 1.0) * glu) @ w_down[e] + b_down[e]

    y = lax.map(expert_block, (block_expert, row_token.reshape(n_blocks, MOE_ROW_BLOCK)))
    y = y.reshape(n_rows, D) * row_gate[:, None]
    out = jnp.zeros((T, D), x.dtype).at[row_token].add(y)
    return out.reshape(B, S, D)


def setup_inputs(seed: int = 0) -> dict:
    key = jax.random.key(seed)
    ks = jax.random.split(key, 26)

    def nrm(k, shape, scale):
        return jax.random.normal(k, shape, jnp.float32) * scale

    def gain(k, shape):
        return 1.0 + nrm(k, shape, 0.05)

    a0 = jax.random.uniform(ks[10], (DEPTH, D_LRU), jnp.float32, minval=0.9, maxval=0.999)
    return {
        "x": nrm(ks[0], (BATCH, SEQ, D_MODEL), 1.0),
        "p": nrm(ks[1], (DEPTH, BATCH, SEQ, D_PLE), 1.0),
        "mix_norm_g": gain(ks[2], (DEPTH, D_MODEL)),
        "w_in": nrm(ks[3], (DEPTH, D_MODEL, D_IN_PROJ), D_MODEL ** -0.5),
        "conv_w": nrm(ks[4], (DEPTH, CONV_WIDTH, D_LRU), CONV_WIDTH ** -0.5),
        "conv_b": nrm(ks[5], (DEPTH, D_LRU), 0.01),
        "lru_w_a": nrm(ks[6], (DEPTH, LRU_BLOCKS, LRU_BLOCK_DIM, LRU_BLOCK_DIM), LRU_BLOCK_DIM ** -0.5),
        "lru_b_a": nrm(ks[7], (DEPTH, D_LRU), 0.01),
        "lru_w_x": nrm(ks[8], (DEPTH, LRU_BLOCKS, LRU_BLOCK_DIM, LRU_BLOCK_DIM), LRU_BLOCK_DIM ** -0.5),
        "lru_b_x": nrm(ks[9], (DEPTH, D_LRU), 0.01),
        "lru_lambda": jnp.log(a0) - jnp.log1p(-a0),
        "lru_out_g": gain(ks[11], (DEPTH, D_LRU)),
        "sb_out_g": gain(ks[12], (DEPTH, D_SB)),
        "w_out": nrm(ks[13], (DEPTH, D_MIX, D_MODEL), D_MIX ** -0.5),
        "ffn_norm_g": gain(ks[14], (DEPTH, D_MODEL)),
        "w_router": nrm(ks[15], (DEPTH, D_MODEL, N_EXPERTS), D_MODEL ** -0.5),
        "b_router": nrm(ks[16], (DEPTH, N_EXPERTS), 0.01),
        "w_up": nrm(ks[17], (DEPTH, N_EXPERTS, D_MODEL, 2 * D_EXPERT), D_MODEL ** -0.5),
        "b_up": nrm(ks[18], (DEPTH, N_EXPERTS, 2 * D_EXPERT), 0.01),
        "w_down": nrm(ks[19], (DEPTH, N_EXPERTS, D_EXPERT, D_MODEL), D_EXPERT ** -0.5),
        "b_down": nrm(ks[20], (DEPTH, N_EXPERTS, D_MODEL), 0.01),
        "ple_norm_g": gain(ks[21], (DEPTH, D_MODEL)),
        "w_ple_gate": nrm(ks[22], (DEPTH, D_MODEL, D_MODEL), D_MODEL ** -0.5),
        "w_ple": nrm(ks[23], (DEPTH, D_PLE, D_MODEL), D_PLE ** -0.5),
        "final_norm_g": gain(ks[24], (D_MODEL,)),
    }


def reference(x, p, mix_norm_g, w_in, conv_w, conv_b, lru_w_a, lru_b_a, lru_w_x, lru_b_x,
              lru_lambda, lru_out_g, sb_out_g, w_out, ffn_norm_g, w_router, b_router,
              w_up, b_up, w_down, b_down, ple_norm_g, w_ple_gate, w_ple, final_norm_g):
    B, S, _ = x.shape
    h = x
    for i in range(DEPTH):
        xn = rms_norm(h, mix_norm_g[i])
        proj = xn @ w_in[i]
        lru_x, lru_gate, q, k, v = jnp.split(
            proj, [D_LRU, 2 * D_LRU, 2 * D_LRU + D_SB, 2 * D_LRU + 2 * D_SB], axis=-1)
        lru_h = rg_lru(causal_depthwise_conv(lru_x, conv_w[i], conv_b[i]),
                       lru_w_a[i], lru_b_a[i], lru_w_x[i], lru_b_x[i], lru_lambda[i])
        lru_y = lru_h * jax.nn.gelu(lru_gate)
        sb_y = stick_breaking_attention(q.reshape(B, S, N_SB_HEADS, SB_HEAD_DIM),
                                        k.reshape(B, S, N_SB_HEADS, SB_HEAD_DIM),
                                        v.reshape(B, S, N_SB_HEADS, SB_HEAD_DIM))
        mixed = jnp.concatenate([rms_norm(lru_y, lru_out_g[i]), rms_norm(sb_y, sb_out_g[i])], axis=-1)
        h = h + mixed @ w_out[i]
        h = h + moe_ffn(rms_norm(h, ffn_norm_g[i]), w_router[i], b_router[i],
                        w_up[i], b_up[i], w_down[i], b_down[i])
        ple = p[i] @ w_ple[i]
        ple_gate = jax.nn.sigmoid(rms_norm(h, ple_norm_g[i]) @ w_ple_gate[i])
        h = h + ple * ple_gate
    return rms_norm(h, final_norm_g)
```

```python
import functools

import jax
import jax.numpy as jnp
from jax import lax
from jax.experimental import pallas as pl
from jax.experimental.pallas import tpu as pltpu

F32 = jnp.float32
BF16 = jnp.bfloat16
I32 = jnp.int32

RMS_EPS = 1e-6
LANES = 128
SUBLANES = 8
D_LRU = 512
D_SB = 512
N_HEADS = 8
HEAD_DIM = 64
CONV_WIDTH = 4
LRU_C = 8.0
N_EXPERTS = 32
TOP_K = 4
SWIGLU_LIMIT = 7.0
SWIGLU_ALPHA = 1.702
ROW_BLOCK = 256
VMEM_LIMIT = 56 * 1024 * 1024


def _rms(x, g):
    return (x * lax.rsqrt(jnp.mean(x * x, axis=-1, keepdims=True) + RMS_EPS)) * g


def _cparams(sem):
    return pltpu.CompilerParams(dimension_semantics=sem, vmem_limit_bytes=VMEM_LIMIT)


def _in_proj_kernel(x_ref, g_ref, w_ref, lx_ref, lg_ref, q_ref, k_ref, v_ref):
    xn = _rms(x_ref[...], g_ref[...])
    proj = jnp.dot(xn.astype(BF16), w_ref[...], preferred_element_type=F32)
    lx_ref[...] = proj[:, 0:D_LRU]
    lg_ref[...] = proj[:, D_LRU:2 * D_LRU]
    o = 2 * D_LRU
    q_ref[...] = proj[:, o:o + D_SB].astype(BF16)
    k_ref[...] = proj[:, o + D_SB:o + 2 * D_SB].astype(BF16)
    v_ref[...] = proj[:, o + 2 * D_SB:o + 3 * D_SB].astype(BF16)


def _in_proj(x2, g, w_bf, tm=512):
    t, d = x2.shape
    n = w_bf.shape[1]
    row = lambda i: (i, 0)
    const = lambda i: (0, 0)
    return pl.pallas_call(
        _in_proj_kernel,
        grid=(t // tm,),
        in_specs=[pl.BlockSpec((tm, d), row), pl.BlockSpec((1, d), const), pl.BlockSpec((d, n), const)],
        out_specs=[pl.BlockSpec((tm, D_LRU), row)] * 2 + [pl.BlockSpec((tm, D_SB), row)] * 3,
        out_shape=[jax.ShapeDtypeStruct((t, D_LRU), F32)] * 2 + [jax.ShapeDtypeStruct((t, D_SB), BF16)] * 3,
        compiler_params=_cparams(("parallel",)),
        name="in_proj",
    )(x2, g, w_bf)


def _shift_rows(x, k, fill):
    rolled = pltpu.roll(x, k, 0)
    rows = lax.broadcasted_iota(I32, x.shape, 0)
    return jnp.where(rows >= k, rolled, fill)


def _lru_kernel(lx_ref, lg_ref, cw_ref, cb_ref, wa_ref, ba_ref, wx_ref, bx_ref, lam_ref, og_ref,
                out_ref, tail_ref, h_ref):
    ts = lx_ref.shape[1]

    @pl.when(pl.program_id(1) == 0)
    def _():
        tail_ref[...] = jnp.zeros_like(tail_ref)
        h_ref[...] = jnp.zeros_like(h_ref)

    x = lx_ref[0]
    tail = tail_ref[...]
    rows = lax.broadcasted_iota(I32, x.shape, 0)
    cw = cw_ref[...]
    conv = x * cw[CONV_WIDTH - 1:CONV_WIDTH, :] + cb_ref[...]
    for k in range(1, CONV_WIDTH):
        cur = pltpu.roll(x, k, 0)
        prev = pltpu.roll(tail, k, 0)
        prev_full = jnp.concatenate([prev] + [prev] * (ts // SUBLANES - 1), axis=0)
        shifted = jnp.where(rows >= k, cur, prev_full)
        conv = conv + shifted * cw[CONV_WIDTH - 1 - k:CONV_WIDTH - k, :]
    tail_ref[...] = x[ts - SUBLANES:, :]

    cb16 = conv.astype(BF16)
    r = jax.nn.sigmoid(jnp.dot(cb16, wa_ref[...], preferred_element_type=F32) + ba_ref[...])
    gi = jax.nn.sigmoid(jnp.dot(cb16, wx_ref[...], preferred_element_type=F32) + bx_ref[...])
    lam = lam_ref[...]
    softplus_neg = jnp.maximum(-lam, 0.0) + jnp.log1p(jnp.exp(-jnp.abs(lam)))
    log_a = (-LRU_C * r) * softplus_neg
    a = jnp.exp(log_a)
    b = jnp.sqrt(1.0 - jnp.exp(2.0 * log_a)) * (gi * conv)

    k = 1
    while k < ts:
        a_sh = _shift_rows(a, k, 1.0)
        b_sh = _shift_rows(b, k, 0.0)
        b = a * b_sh + b
        a = a * a_sh
        k *= 2
    h = a * h_ref[0:1, :] + b
    h_ref[...] = jnp.broadcast_to(h[ts - 1:ts, :], h_ref.shape)

    gate = lg_ref[0]
    y = h * jax.nn.gelu(gate)
    out_ref[0] = _rms(y, og_ref[...]).astype(out_ref.dtype)


def _lru(lx, lg, cw, cb, wa, ba, wx, bx, lam, og, ts=256):
    b, s, c = lx.shape
    tile = lambda i, j: (i, j, 0)
    const = lambda i, j: (0, 0)
    vec = pl.BlockSpec((1, c), const)
    return pl.pallas_call(
        _lru_kernel,
        grid=(b, s // ts),
        in_specs=[pl.BlockSpec((1, ts, c), tile), pl.BlockSpec((1, ts, c), tile),
                  pl.BlockSpec((CONV_WIDTH, c), const), vec,
                  pl.BlockSpec((c, c), const), vec, pl.BlockSpec((c, c), const), vec, vec, vec],
        out_specs=pl.BlockSpec((1, ts, c), tile),
        out_shape=jax.ShapeDtypeStruct((b, s, c), BF16),
        scratch_shapes=[pltpu.VMEM((SUBLANES, c), F32), pltpu.VMEM((SUBLANES, c), F32)],
        compiler_params=_cparams(("parallel", "arbitrary")),
        name="lru",
    )(lx, lg, cw, cb, wa, ba, wx, bx, lam, og)


def _split_bf16(x):
    hi = x.astype(BF16)
    lo = (x - hi.astype(F32)).astype(BF16)
    return hi, lo


def _sb_attn_kernel(q_ref, k_ref, v_ref, o_ref, acc_ref, run_ref, *, tq, tk):
    qi = pl.program_id(2)
    lane = lax.broadcasted_iota(I32, (1, LANES), 1)
    head_masks = [lane < HEAD_DIM, lane >= HEAD_DIM]
    scale = HEAD_DIM ** -0.5
    q = q_ref[0]
    zero = jnp.zeros((), BF16)
    qs = [jnp.where(m, q, zero) for m in head_masks]

    r = lax.broadcasted_iota(I32, (tk, tk + LANES), 0)
    c = lax.broadcasted_iota(I32, (tk, tk + LANES), 1)
    suffix = jnp.where((r > c) | (c >= tk), 1.0, 0.0).astype(BF16)

    acc_ref[...] = jnp.zeros_like(acc_ref)
    run_ref[...] = jnp.zeros_like(run_ref)

    def tile(j, diagonal):
        ks = k_ref[0, pl.ds(pl.multiple_of(j * tk, tk), tk), :]
        vs = v_ref[0, pl.ds(pl.multiple_of(j * tk, tk), tk), :]
        if diagonal:
            qpos = lax.broadcasted_iota(I32, (tq, tk), 0)
            kpos = lax.broadcasted_iota(I32, (tq, tk), 1)
            visible = kpos < qpos
        for hd in range(2):
            z = lax.dot_general(qs[hd], ks, (((1,), (1,)), ((), ())), preferred_element_type=F32) * scale
            sp = jnp.maximum(z, 0.0) + jnp.log(1.0 + jnp.exp(-jnp.abs(z)))
            log_keep = -sp
            log_beta = z - sp
            if diagonal:
                log_keep = jnp.where(visible, log_keep, 0.0)
            hi, lo = _split_bf16(log_keep)
            sums = (jnp.dot(hi, suffix, preferred_element_type=F32)
                    + jnp.dot(lo, suffix, preferred_element_type=F32))
            run = run_ref[hd]
            between = sums[:, :tk] + jnp.concatenate([run] * (tk // LANES), axis=1)
            w = jnp.exp(log_beta + between)
            if diagonal:
                w = jnp.where(visible, w, 0.0)
            vh = jnp.where(head_masks[hd], vs, zero)
            acc_ref[...] += jnp.dot(w.astype(BF16), vh, preferred_element_type=F32)
            run_ref[hd] = run + sums[:, tk:]

    tile(qi, True)

    def body(i, carry):
        tile(qi - 1 - i, False)
        return carry

    lax.fori_loop(0, qi, body, 0)
    o_ref[0] = acc_ref[...]


def _sb_attn(q, k, v, tq=256):
    b, s, c = q.shape
    tk = tq
    qspec = pl.BlockSpec((1, tq, LANES), lambda i, j, l: (i, l, j))
    kvspec = pl.BlockSpec((1, s, LANES), lambda i, j, l: (i, 0, j))
    return pl.pallas_call(
        functools.partial(_sb_attn_kernel, tq=tq, tk=tk),
        grid=(b, c // LANES, s // tq),
        in_specs=[qspec, kvspec, kvspec],
        out_specs=qspec,
        out_shape=jax.ShapeDtypeStruct((b, s, c), F32),
        scratch_shapes=[pltpu.VMEM((tq, LANES), F32), pltpu.VMEM((2, tq, LANES), F32)],
        compiler_params=_cparams(("parallel", "parallel", "arbitrary")),
        name="sb_attn",
    )(q, k, v)


def _out_route_kernel(x_ref, lru_ref, sb_ref, sbg_ref, wol_ref, wos_ref, fg_ref, wr_ref, br_ref,
                      h_ref, xn3_ref, eidx_ref, gate_ref, rank_ref, cnt_ref, cnt_scr):
    tm = x_ref.shape[0]

    @pl.when(pl.program_id(0) == 0)
    def _():
        cnt_scr[...] = jnp.zeros_like(cnt_scr)

    sbn = _rms(sb_ref[...], sbg_ref[...]).astype(BF16)
    h = (x_ref[...] + jnp.dot(lru_ref[...], wol_ref[...], preferred_element_type=F32)
         + jnp.dot(sbn, wos_ref[...], preferred_element_type=F32))
    h_ref[...] = h
    xn = _rms(h, fg_ref[...])
    for j in range(xn.shape[1] // LANES):
        xn3_ref[pl.ds(j, tm, stride=SUBLANES), :] = xn[:, j * LANES:(j + 1) * LANES]

    logits = lax.dot_general(wr_ref[...], xn, (((1,), (1,)), ((), ())),
                             precision=lax.Precision.HIGHEST, preferred_element_type=F32) + br_ref[...]
    n_exp = logits.shape[0]
    eio = lax.broadcasted_iota(I32, logits.shape, 0)
    work = logits
    vals, hits, idxs = [], [], []
    for _k in range(TOP_K):
        m = jnp.max(work, axis=0, keepdims=True)
        idx = jnp.min(jnp.where(work == m, eio, n_exp), axis=0, keepdims=True)
        hit = eio == idx
        work = jnp.where(hit, -jnp.inf, work)
        vals.append(m)
        hits.append(hit)
        idxs.append(idx)
    exps = [jnp.exp(v - vals[0]) for v in vals]
    denom = exps[0] + exps[1] + exps[2] + exps[3]
    gates = [e / denom for e in exps]

    onehot = jnp.where(hits[0] | hits[1] | hits[2] | hits[3], 1.0, 0.0).astype(BF16)
    r = lax.broadcasted_iota(I32, (tm, tm + LANES), 0)
    c = lax.broadcasted_iota(I32, (tm, tm + LANES), 1)
    prefix_mat = jnp.where((r < c) | (c >= tm), 1.0, 0.0).astype(BF16)
    sums = jnp.dot(onehot, prefix_mat, preferred_element_type=F32)
    base = cnt_scr[...]
    pos = sums[:, :tm] + jnp.concatenate([base] * (tm // LANES), axis=1)
    ranks = [jnp.sum(jnp.where(hk, pos, 0.0), axis=0, keepdims=True) for hk in hits]
    cnt_scr[...] = base + sums[:, tm:]
    cnt_ref[...] = cnt_scr[...]

    pad_i = jnp.zeros((SUBLANES - TOP_K, tm), I32)
    eidx_ref[...] = jnp.concatenate(idxs + [pad_i], axis=0)
    rank_ref[...] = jnp.concatenate([rk.astype(I32) for rk in ranks] + [pad_i], axis=0)
    gate_ref[...] = jnp.concatenate(gates + [jnp.zeros((SUBLANES - TOP_K, tm), F32)], axis=0)


def _out_route(x2, lru_n, sb_y, sbg, wol, wos, fg, wr_t, br, tm=256):
    t, d = x2.shape
    e = wr_t.shape[0]
    row = lambda i: (i, 0)
    col = lambda i: (0, i)
    const = lambda i: (0, 0)
    meta = pl.BlockSpec((SUBLANES, tm), col)
    return pl.pallas_call(
        _out_route_kernel,
        grid=(t // tm,),
        in_specs=[pl.BlockSpec((tm, d), row), pl.BlockSpec((tm, D_LRU), row), pl.BlockSpec((tm, D_SB), row),
                  pl.BlockSpec((1, D_SB), const), pl.BlockSpec((D_LRU, d), const), pl.BlockSpec((D_SB, d), const),
                  pl.BlockSpec((1, d), const), pl.BlockSpec((e, d), const), pl.BlockSpec((e, 1), const)],
        out_specs=[pl.BlockSpec((tm, d), row), pl.BlockSpec((tm * SUBLANES, LANES), row), meta, meta, meta,
                   pl.BlockSpec((e, LANES), const)],
        out_shape=[jax.ShapeDtypeStruct((t, d), F32), jax.ShapeDtypeStruct((t * SUBLANES, LANES), F32),
                   jax.ShapeDtypeStruct((SUBLANES, t), I32), jax.ShapeDtypeStruct((SUBLANES, t), F32),
                   jax.ShapeDtypeStruct((SUBLANES, t), I32), jax.ShapeDtypeStruct((e, LANES), F32)],
        scratch_shapes=[pltpu.VMEM((e, LANES), F32)],
        compiler_params=_cparams(("arbitrary",)),
        name="out_route",
    )(x2, lru_n, sb_y, sbg, wol, wos, fg, wr_t, br)


def _row_tile(ref, row):
    return ref.at[pl.ds(pl.multiple_of(row * SUBLANES, SUBLANES), SUBLANES)]


def _dispatch_kernel(pstart_ref, eidx_ref, rank_ref, xn3_hbm, xs_in, dest_ref, xs_out,
                     dest_vmem, dest_smem, sem_idx, sem_rows):
    del xs_in
    tm = eidx_ref.shape[1]
    e = eidx_ref[...]
    start = jnp.zeros_like(e)
    for ex in range(N_EXPERTS):
        start = jnp.where(e == ex, pstart_ref[ex], start)
    dest = rank_ref[...] + start
    dest_ref[...] = dest
    dest_vmem[...] = dest
    to_smem = pltpu.make_async_copy(dest_vmem, dest_smem, sem_idx)
    to_smem.start()
    to_smem.wait()

    base = pl.program_id(0) * tm

    def row_copy(tok, k):
        return pltpu.make_async_copy(_row_tile(xn3_hbm, base + tok), _row_tile(xs_out, dest_smem[k, tok]), sem_rows)

    def issue(tok, carry):
        for k in range(TOP_K):
            row_copy(tok, k).start()
        return carry

    lax.fori_loop(0, tm, issue, 0, unroll=8)

    def drain(tok, carry):
        for k in range(TOP_K):
            row_copy(tok, k).wait()
        return carry

    lax.fori_loop(0, tm, drain, 0, unroll=8)


def _dispatch(pstart, eidx, ranks, xn3, n_rows, tm=256):
    t = eidx.shape[1]
    meta = pl.BlockSpec((SUBLANES, tm), lambda i, ps: (0, i))
    any_spec = pl.BlockSpec(memory_space=pl.ANY)
    xs0 = jnp.zeros((n_rows * SUBLANES, LANES), F32)
    grid_spec = pltpu.PrefetchScalarGridSpec(
        num_scalar_prefetch=1,
        grid=(t // tm,),
        in_specs=[meta, meta, any_spec, any_spec],
        out_specs=[meta, any_spec],
        scratch_shapes=[pltpu.VMEM((SUBLANES, tm), I32), pltpu.SMEM((SUBLANES, tm), I32),
                        pltpu.SemaphoreType.DMA, pltpu.SemaphoreType.DMA],
    )
    return pl.pallas_call(
        _dispatch_kernel,
        grid_spec=grid_spec,
        out_shape=[jax.ShapeDtypeStruct((SUBLANES, t), I32), jax.ShapeDtypeStruct(xs0.shape, F32)],
        input_output_aliases={4: 1},
        compiler_params=_cparams(("arbitrary",)),
        name="dispatch",
    )(pstart, eidx, ranks, xn3, xs0)


def _experts_kernel(bexp_ref, nblk_ref, xs_ref, wup_ref, bup_ref, wdn_ref, bdn_ref, y_ref,
                    wup16, wdn16, x16):
    i = pl.program_id(0)
    rows = x16.shape[0]
    d = x16.shape[1]
    de = wdn16.shape[0]
    prev = bexp_ref[jnp.maximum(i - 1, 0)]
    fresh = (i == 0) | (bexp_ref[i] != prev)

    @pl.when(fresh)
    def _():
        wup16[...] = wup_ref[0].astype(BF16)
        wdn16[...] = wdn_ref[0].astype(BF16)

    @pl.when(i < nblk_ref[0])
    def _():
        for j in range(d // LANES):
            x16[:, j * LANES:(j + 1) * LANES] = xs_ref[pl.ds(j, rows, stride=SUBLANES), :].astype(BF16)
        hdn = jnp.dot(x16[...], wup16[...], preferred_element_type=F32) + bup_ref[0]
        g = jnp.minimum(hdn[:, :de], SWIGLU_LIMIT)
        u = jnp.clip(hdn[:, de:], -SWIGLU_LIMIT, SWIGLU_LIMIT)
        glu = g * jax.nn.sigmoid(SWIGLU_ALPHA * g)
        act = ((u + 1.0) * glu).astype(BF16)
        y = jnp.dot(act, wdn16[...], preferred_element_type=F32) + bdn_ref[0]
        for j in range(d // LANES):
            y_ref[pl.ds(j, rows, stride=SUBLANES), :] = y[:, j * LANES:(j + 1) * LANES]

    @pl.when(i >= nblk_ref[0])
    def _():
        y_ref[...] = jnp.zeros_like(y_ref)


def _experts(bexp, nblk, xs, w_up, b_up, w_down, b_down):
    n_exp, d, d2 = w_up.shape
    de = w_down.shape[1]
    n_blocks = bexp.shape[0]
    rows = ROW_BLOCK
    xspec = pl.BlockSpec((rows * SUBLANES, LANES), lambda i, be, nb: (i, 0))
    grid_spec = pltpu.PrefetchScalarGridSpec(
        num_scalar_prefetch=2,
        grid=(n_blocks,),
        in_specs=[xspec,
                  pl.BlockSpec((1, d, d2), lambda i, be, nb: (be[i], 0, 0)),
                  pl.BlockSpec((1, 1, d2), lambda i, be, nb: (be[i], 0, 0)),
                  pl.BlockSpec((1, de, d), lambda i, be, nb: (be[i], 0, 0)),
                  pl.BlockSpec((1, 1, d), lambda i, be, nb: (be[i], 0, 0))],
        out_specs=xspec,
        scratch_shapes=[pltpu.VMEM((d, d2), BF16), pltpu.VMEM((de, d), BF16), pltpu.VMEM((rows, d), BF16)],
    )
    return pl.pallas_call(
        _experts_kernel,
        grid_spec=grid_spec,
        out_shape=jax.ShapeDtypeStruct(xs.shape, F32),
        compiler_params=_cparams(("arbitrary",)),
        name="experts",
    )(bexp, nblk, xs, w_up, b_up[:, None, :], w_down, b_down[:, None, :])


def _combine_kernel(dest_ref, gate_ref, h_ref, p_ref, pg_ref, wpg_ref, wp_ref, fg_ref, y_hbm,
                    out_ref, dest_smem, ybuf, sem_idx, sem_rows):
    tm = h_ref.shape[0]
    d = h_ref.shape[1]
    to_smem = pltpu.make_async_copy(dest_ref, dest_smem, sem_idx)
    to_smem.start()
    to_smem.wait()

    def row_copy(tok, k):
        return pltpu.make_async_copy(_row_tile(y_hbm, dest_smem[k, tok]), _row_tile(ybuf.at[k], tok), sem_rows)

    def issue(tok, carry):
        for k in range(TOP_K):
            row_copy(tok, k).start()
        return carry

    lax.fori_loop(0, tm, issue, 0, unroll=8)

    def drain(tok, carry):
        for k in range(TOP_K):
            row_copy(tok, k).wait()
        return carry

    lax.fori_loop(0, tm, drain, 0, unroll=8)

    gates_t = jnp.transpose(gate_ref[...])
    pieces = []
    for j in range(d // LANES):
        acc = jnp.zeros((tm, LANES), F32)
        for k in range(TOP_K):
            acc = acc + gates_t[:, k:k + 1] * ybuf[k, pl.ds(j, tm, stride=SUBLANES), :]
        pieces.append(acc)
    h = h_ref[...] + jnp.concatenate(pieces, axis=1)

    ple = jnp.dot(p_ref[...].astype(BF16), wp_ref[...], preferred_element_type=F32)
    ple_gate = jax.nn.sigmoid(jnp.dot(_rms(h, pg_ref[...]).astype(BF16), wpg_ref[...],
                                      preferred_element_type=F32))
    h = h + ple * ple_gate
    out_ref[...] = _rms(h, fg_ref[...])


def _combine(dest, gates, h1, p2, pg, wpg, wp, fg, y, tm=256):
    t, d = h1.shape
    dp = p2.shape[1]
    row = lambda i: (i, 0)
    col = lambda i: (0, i)
    const = lambda i: (0, 0)
    return pl.pallas_call(
        _combine_kernel,
        grid=(t // tm,),
        in_specs=[pl.BlockSpec((SUBLANES, tm), col), pl.BlockSpec((SUBLANES, tm), col),
                  pl.BlockSpec((tm, d), row), pl.BlockSpec((tm, dp), row),
                  pl.BlockSpec((1, d), const), pl.BlockSpec((d, d), const), pl.BlockSpec((dp, d), const),
                  pl.BlockSpec((1, d), const), pl.BlockSpec(memory_space=pl.ANY)],
        out_specs=pl.BlockSpec((tm, d), row),
        out_shape=jax.ShapeDtypeStruct((t, d), F32),
        scratch_shapes=[pltpu.SMEM((SUBLANES, tm), I32), pltpu.VMEM((TOP_K, tm * SUBLANES, LANES), F32),
                        pltpu.SemaphoreType.DMA, pltpu.SemaphoreType.DMA],
        compiler_params=_cparams(("arbitrary",)),
        name="combine",
    )(dest, gates, h1, p2, pg, wpg, wp, fg, y)


def _block_diag(w):
    n, d, _ = w.shape
    eye = jnp.eye(n, dtype=w.dtype)
    return (eye[:, None, :, None] * w[:, :, None, :]).reshape(n * d, n * d)


def _stages(x, p, mix_norm_g, w_in, conv_w, conv_b, lru_w_a, lru_b_a, lru_w_x, lru_b_x, lru_lambda,
            lru_out_g, sb_out_g, w_out, ffn_norm_g, w_router, b_router, w_up, b_up, w_down, b_down,
            ple_norm_g, w_ple_gate, w_ple, final_norm_g):
    b, s, d = x.shape
    t = b * s
    st = {}
    x2 = x.reshape(t, d)
    lx, lg, q, k, v = _in_proj(x2, mix_norm_g[0][None], w_in[0].astype(BF16))
    st.update(lru_x=lx, lru_gate=lg, q=q, k=k, v=v)
    row = lambda a: a[None].astype(F32)
    lru_n = _lru(lx.reshape(b, s, D_LRU), lg.reshape(b, s, D_LRU), conv_w[0], row(conv_b[0]),
                 _block_diag(lru_w_a[0]).astype(BF16), row(lru_b_a[0]),
                 _block_diag(lru_w_x[0]).astype(BF16), row(lru_b_x[0]),
                 row(lru_lambda[0]), row(lru_out_g[0]))
    st["lru_n"] = lru_n
    sb_y = _sb_attn(q.reshape(b, s, D_SB), k.reshape(b, s, D_SB), v.reshape(b, s, D_SB))
    st["sb_y"] = sb_y
    wo = w_out[0].astype(BF16)
    h1, xn3, eidx, gates, ranks, counts = _out_route(
        x2, lru_n.reshape(t, D_LRU), sb_y.reshape(t, D_SB), row(sb_out_g[0]), wo[:D_LRU], wo[D_LRU:],
        row(ffn_norm_g[0]), w_router[0].T, b_router[0][:, None])
    st.update(h1=h1, xn1=xn3, eidx=eidx, gates=gates, ranks=ranks, counts=counts)

    cnt = counts[:, 0].astype(I32)
    padded = (cnt + ROW_BLOCK - 1) // ROW_BLOCK * ROW_BLOCK
    pend = jnp.cumsum(padded)
    pstart = pend - padded
    n_blocks = -(-(t * TOP_K) // ROW_BLOCK) + N_EXPERTS
    block_row0 = jnp.arange(n_blocks, dtype=I32) * ROW_BLOCK
    bexp = jnp.minimum(jnp.sum((pend[None, :] <= block_row0[:, None]).astype(I32), axis=1), N_EXPERTS - 1)
    nblk = (pend[-1:] // ROW_BLOCK).astype(I32)

    dest, xs = _dispatch(pstart, eidx, ranks, xn3, n_blocks * ROW_BLOCK)
    y = _experts(bexp, nblk, xs, w_up[0], b_up[0], w_down[0], b_down[0])
    st.update(dest=dest, xs=xs, y=y)
    out = _combine(dest, gates, h1, p[0].reshape(t, -1), row(ple_norm_g[0]), w_ple_gate[0].astype(BF16),
                   w_ple[0].astype(BF16), row(final_norm_g), y)
    st["final"] = out.reshape(b, s, d)
    return st


def kernel(x, p, mix_norm_g, w_in, conv_w, conv_b, lru_w_a, lru_b_a, lru_w_x, lru_b_x, lru_lambda,
           lru_out_g, sb_out_g, w_out, ffn_norm_g, w_router, b_router, w_up, b_up, w_down, b_down,
           ple_norm_g, w_ple_gate, w_ple, final_norm_g):
    return _stages(x, p, mix_norm_g, w_in, conv_w, conv_b, lru_w_a, lru_b_a, lru_w_x, lru_b_x, lru_lambda,
                   lru_out_g, sb_out_g, w_out, ffn_norm_g, w_router, b_router, w_up, b_up, w_down, b_down,
                   ple_norm_g, w_ple_gate, w_ple, final_norm_g)["final"]
```

```python
import functools

import jax
import jax.numpy as jnp
from jax import lax
from jax.experimental import pallas as pl
from jax.experimental.pallas import tpu as pltpu

F32 = jnp.float32
BF16 = jnp.bfloat16
I32 = jnp.int32

RMS_EPS = 1e-6
LANES = 128
SUBLANES = 8
D_LRU = 512
D_SB = 512
N_HEADS = 8
HEAD_DIM = 64
CONV_WIDTH = 4
LRU_C = 8.0
N_EXPERTS = 32
TOP_K = 4
SWIGLU_LIMIT = 7.0
SWIGLU_ALPHA = 1.702
ROW_BLOCK = 256
VMEM_LIMIT = 56 * 1024 * 1024


def _rms(x, g):
    return (x * lax.rsqrt(jnp.mean(x * x, axis=-1, keepdims=True) + RMS_EPS)) * g


def _cparams(sem):
    return pltpu.CompilerParams(dimension_semantics=sem, vmem_limit_bytes=VMEM_LIMIT)


def _in_proj_kernel(x_ref, g_ref, w_ref, lx_ref, lg_ref, q_ref, k_ref, v_ref):
    xn = _rms(x_ref[...], g_ref[...])
    proj = jnp.dot(xn.astype(BF16), w_ref[...], preferred_element_type=F32)
    lx_ref[...] = proj[:, 0:D_LRU]
    lg_ref[...] = proj[:, D_LRU:2 * D_LRU]
    o = 2 * D_LRU
    q_ref[...] = proj[:, o:o + D_SB].astype(BF16)
    k_ref[...] = proj[:, o + D_SB:o + 2 * D_SB].astype(BF16)
    v_ref[...] = proj[:, o + 2 * D_SB:o + 3 * D_SB].astype(BF16)


def _in_proj(x2, g, w_bf, tm=512):
    t, d = x2.shape
    n = w_bf.shape[1]
    row = lambda i: (i, 0)
    const = lambda i: (0, 0)
    return pl.pallas_call(
        _in_proj_kernel,
        grid=(t // tm,),
        in_specs=[pl.BlockSpec((tm, d), row), pl.BlockSpec((1, d), const), pl.BlockSpec((d, n), const)],
        out_specs=[pl.BlockSpec((tm, D_LRU), row)] * 2 + [pl.BlockSpec((tm, D_SB), row)] * 3,
        out_shape=[jax.ShapeDtypeStruct((t, D_LRU), F32)] * 2 + [jax.ShapeDtypeStruct((t, D_SB), BF16)] * 3,
        compiler_params=_cparams(("parallel",)),
        name="in_proj",
    )(x2, g, w_bf)


def _shift_rows(x, k, fill):
    rolled = pltpu.roll(x, k, 0)
    rows = lax.broadcasted_iota(I32, x.shape, 0)
    return jnp.where(rows >= k, rolled, fill)


def _lru_kernel(lx_ref, lg_ref, cw_ref, cb_ref, wa_ref, ba_ref, wx_ref, bx_ref, lam_ref, og_ref,
                out_ref, tail_ref, h_ref):
    ts = lx_ref.shape[1]

    @pl.when(pl.program_id(1) == 0)
    def _():
        tail_ref[...] = jnp.zeros_like(tail_ref)
        h_ref[...] = jnp.zeros_like(h_ref)

    x = lx_ref[0]
    tail = tail_ref[...]
    rows = lax.broadcasted_iota(I32, x.shape, 0)
    cw = cw_ref[...]
    conv = x * cw[CONV_WIDTH - 1:CONV_WIDTH, :] + cb_ref[...]
    for k in range(1, CONV_WIDTH):
        cur = pltpu.roll(x, k, 0)
        prev = pltpu.roll(tail, k, 0)
        prev_full = jnp.concatenate([prev] + [prev] * (ts // SUBLANES - 1), axis=0)
        shifted = jnp.where(rows >= k, cur, prev_full)
        conv = conv + shifted * cw[CONV_WIDTH - 1 - k:CONV_WIDTH - k, :]
    tail_ref[...] = x[ts - SUBLANES:, :]

    cb16 = conv.astype(BF16)
    r = jax.nn.sigmoid(jnp.dot(cb16, wa_ref[...], preferred_element_type=F32) + ba_ref[...])
    gi = jax.nn.sigmoid(jnp.dot(cb16, wx_ref[...], preferred_element_type=F32) + bx_ref[...])
    lam = lam_ref[...]
    softplus_neg = jnp.maximum(-lam, 0.0) + jnp.log1p(jnp.exp(-jnp.abs(lam)))
    log_a = (-LRU_C * r) * softplus_neg
    a = jnp.exp(log_a)
    b = jnp.sqrt(1.0 - jnp.exp(2.0 * log_a)) * (gi * conv)

    k = 1
    while k < ts:
        a_sh = _shift_rows(a, k, 1.0)
        b_sh = _shift_rows(b, k, 0.0)
        b = a * b_sh + b
        a = a * a_sh
        k *= 2
    h = a * h_ref[0:1, :] + b
    h_ref[...] = jnp.broadcast_to(h[ts - 1:ts, :], h_ref.shape)

    gate = lg_ref[0]
    y = h * jax.nn.gelu(gate)
    out_ref[0] = _rms(y, og_ref[...]).astype(out_ref.dtype)


def _lru(lx, lg, cw, cb, wa, ba, wx, bx, lam, og, ts=256):
    b, s, c = lx.shape
    tile = lambda i, j: (i, j, 0)
    const = lambda i, j: (0, 0)
    vec = pl.BlockSpec((1, c), const)
    return pl.pallas_call(
        _lru_kernel,
        grid=(b, s // ts),
        in_specs=[pl.BlockSpec((1, ts, c), tile), pl.BlockSpec((1, ts, c), tile),
                  pl.BlockSpec((CONV_WIDTH, c), const), vec,
                  pl.BlockSpec((c, c), const), vec, pl.BlockSpec((c, c), const), vec, vec, vec],
        out_specs=pl.BlockSpec((1, ts, c), tile),
        out_shape=jax.ShapeDtypeStruct((b, s, c), BF16),
        scratch_shapes=[pltpu.VMEM((SUBLANES, c), F32), pltpu.VMEM((SUBLANES, c), F32)],
        compiler_params=_cparams(("parallel", "arbitrary")),
        name="lru",
    )(lx, lg, cw, cb, wa, ba, wx, bx, lam, og)


def _split_bf16(x):
    hi = x.astype(BF16)
    lo = (x - hi.astype(F32)).astype(BF16)
    return hi, lo


def _sb_attn_kernel(q_ref, k_ref, v_ref, o_ref, acc_ref, run_ref, *, tq, tk):
    qi = pl.program_id(2)
    lane = lax.broadcasted_iota(I32, (1, LANES), 1)
    head_masks = [lane < HEAD_DIM, lane >= HEAD_DIM]
    scale = HEAD_DIM ** -0.5
    q = q_ref[0]
    zero = jnp.zeros((), BF16)
    qs = [jnp.where(m, q, zero) for m in head_masks]

    r = lax.broadcasted_iota(I32, (tk, tk + LANES), 0)
    c = lax.broadcasted_iota(I32, (tk, tk + LANES), 1)
    suffix = jnp.where((r > c) | (c >= tk), 1.0, 0.0).astype(BF16)

    acc_ref[...] = jnp.zeros_like(acc_ref)
    run_ref[...] = jnp.zeros_like(run_ref)

    def tile(j, diagonal):
        ks = k_ref[0, pl.ds(pl.multiple_of(j * tk, tk), tk), :]
        vs = v_ref[0, pl.ds(pl.multiple_of(j * tk, tk), tk), :]
        if diagonal:
            qpos = lax.broadcasted_iota(I32, (tq, tk), 0)
            kpos = lax.broadcasted_iota(I32, (tq, tk), 1)
            visible = kpos < qpos
        for hd in range(2):
            z = lax.dot_general(qs[hd], ks, (((1,), (1,)), ((), ())), preferred_element_type=F32) * scale
            sp = jnp.maximum(z, 0.0) + jnp.log(1.0 + jnp.exp(-jnp.abs(z)))
            log_keep = -sp
            log_beta = z - sp
            if diagonal:
                log_keep = jnp.where(visible, log_keep, 0.0)
            hi, lo = _split_bf16(log_keep)
            sums = (jnp.dot(hi, suffix, preferred_element_type=F32)
                    + jnp.dot(lo, suffix, preferred_element_type=F32))
            run = run_ref[hd]
            between = sums[:, :tk] + jnp.concatenate([run] * (tk // LANES), axis=1)
            w = jnp.exp(log_beta + between)
            if diagonal:
                w = jnp.where(visible, w, 0.0)
            vh = jnp.where(head_masks[hd], vs, zero)
            acc_ref[...] += jnp.dot(w.astype(BF16), vh, preferred_element_type=F32)
            run_ref[hd] = run + sums[:, tk:]

    tile(qi, True)

    def body(i, carry):
        tile(qi - 1 - i, False)
        return carry

    lax.fori_loop(0, qi, body, 0)
    o_ref[0] = acc_ref[...]


def _sb_attn(q, k, v, tq=256):
    b, s, c = q.shape
    tk = tq
    qspec = pl.BlockSpec((1, tq, LANES), lambda i, j, l: (i, l, j))
    kvspec = pl.BlockSpec((1, s, LANES), lambda i, j, l: (i, 0, j))
    return pl.pallas_call(
        functools.partial(_sb_attn_kernel, tq=tq, tk=tk),
        grid=(b, c // LANES, s // tq),
        in_specs=[qspec, kvspec, kvspec],
        out_specs=qspec,
        out_shape=jax.ShapeDtypeStruct((b, s, c), F32),
        scratch_shapes=[pltpu.VMEM((tq, LANES), F32), pltpu.VMEM((2, tq, LANES), F32)],
        compiler_params=_cparams(("parallel", "parallel", "arbitrary")),
        name="sb_attn",
    )(q, k, v)


def _out_route_kernel(x_ref, lru_ref, sb_ref, sbg_ref, wol_ref, wos_ref, fg_ref, wr_ref, br_ref,
                      h_ref, xn3_ref, eidx_ref, gate_ref, rank_ref, cnt_ref, cnt_scr):
    tm = x_ref.shape[0]

    @pl.when(pl.program_id(0) == 0)
    def _():
        cnt_scr[...] = jnp.zeros_like(cnt_scr)

    sbn = _rms(sb_ref[...], sbg_ref[...]).astype(BF16)
    h = (x_ref[...] + jnp.dot(lru_ref[...], wol_ref[...], preferred_element_type=F32)
         + jnp.dot(sbn, wos_ref[...], preferred_element_type=F32))
    h_ref[...] = h
    xn = _rms(h, fg_ref[...])
    for j in range(xn.shape[1] // LANES):
        xn3_ref[pl.ds(j, tm, stride=SUBLANES), :] = xn[:, j * LANES:(j + 1) * LANES]

    logits = lax.dot_general(wr_ref[...], xn, (((1,), (1,)), ((), ())),
                             precision=lax.Precision.HIGHEST, preferred_element_type=F32) + br_ref[...]
    n_exp = logits.shape[0]
    eio = lax.broadcasted_iota(I32, logits.shape, 0)
    work = logits
    vals, hits, idxs = [], [], []
    for _k in range(TOP_K):
        m = jnp.max(work, axis=0, keepdims=True)
        idx = jnp.min(jnp.where(work == m, eio, n_exp), axis=0, keepdims=True)
        hit = eio == idx
        work = jnp.where(hit, -jnp.inf, work)
        vals.append(m)
        hits.append(hit)
        idxs.append(idx)
    exps = [jnp.exp(v - vals[0]) for v in vals]
    denom = exps[0] + exps[1] + exps[2] + exps[3]
    gates = [e / denom for e in exps]

    onehot = jnp.where(hits[0] | hits[1] | hits[2] | hits[3], 1.0, 0.0).astype(BF16)
    r = lax.broadcasted_iota(I32, (tm, tm + LANES), 0)
    c = lax.broadcasted_iota(I32, (tm, tm + LANES), 1)
    prefix_mat = jnp.where((r < c) | (c >= tm), 1.0, 0.0).astype(BF16)
    sums = jnp.dot(onehot, prefix_mat, preferred_element_type=F32)
    base = cnt_scr[...]
    pos = sums[:, :tm] + jnp.concatenate([base] * (tm // LANES), axis=1)
    ranks = [jnp.sum(jnp.where(hk, pos, 0.0), axis=0, keepdims=True) for hk in hits]
    cnt_scr[...] = base + sums[:, tm:]
    cnt_ref[...] = cnt_scr[...]

    pad_i = jnp.zeros((SUBLANES - TOP_K, tm), I32)
    eidx_ref[...] = jnp.concatenate(idxs + [pad_i], axis=0)
    rank_ref[...] = jnp.concatenate([rk.astype(I32) for rk in ranks] + [pad_i], axis=0)
    gate_ref[...] = jnp.concatenate(gates + [jnp.zeros((SUBLANES - TOP_K, tm), F32)], axis=0)


def _out_route(x2, lru_n, sb_y, sbg, wol, wos, fg, wr_t, br, tm=256):
    t, d = x2.shape
    e = wr_t.shape[0]
    row = lambda i: (i, 0)
    col = lambda i: (0, i)
    const = lambda i: (0, 0)
    meta = pl.BlockSpec((SUBLANES, tm), col)
    return pl.pallas_call(
        _out_route_kernel,
        grid=(t // tm,),
        in_specs=[pl.BlockSpec((tm, d), row), pl.BlockSpec((tm, D_LRU), row), pl.BlockSpec((tm, D_SB), row),
                  pl.BlockSpec((1, D_SB), const), pl.BlockSpec((D_LRU, d), const), pl.BlockSpec((D_SB, d), const),
                  pl.BlockSpec((1, d), const), pl.BlockSpec((e, d), const), pl.BlockSpec((e, 1), const)],
        out_specs=[pl.BlockSpec((tm, d), row), pl.BlockSpec((tm * SUBLANES, LANES), row), meta, meta, meta,
                   pl.BlockSpec((e, LANES), const)],
        out_shape=[jax.ShapeDtypeStruct((t, d), F32), jax.ShapeDtypeStruct((t * SUBLANES, LANES), F32),
                   jax.ShapeDtypeStruct((SUBLANES, t), I32), jax.ShapeDtypeStruct((SUBLANES, t), F32),
                   jax.ShapeDtypeStruct((SUBLANES, t), I32), jax.ShapeDtypeStruct((e, LANES), F32)],
        scratch_shapes=[pltpu.VMEM((e, LANES), F32)],
        compiler_params=_cparams(("arbitrary",)),
        name="out_route",
    )(x2, lru_n, sb_y, sbg, wol, wos, fg, wr_t, br)


def _row_tile(ref, row):
    return ref.at[pl.ds(pl.multiple_of(row * SUBLANES, SUBLANES), SUBLANES)]


def _dispatch_kernel(pstart_ref, eidx_ref, rank_ref, xn3_ref, xs_in, dest_ref, xs_out,
                     dest_vmem, dest_smem, sem_idx, sem_rows):
    del xs_in
    tm = eidx_ref.shape[1]
    e = eidx_ref[...]
    start = jnp.zeros_like(e)
    for ex in range(N_EXPERTS):
        start = jnp.where(e == ex, pstart_ref[ex], start)
    dest = rank_ref[...] + start
    dest_ref[...] = dest
    dest_vmem[...] = dest
    to_smem = pltpu.make_async_copy(dest_vmem, dest_smem, sem_idx)
    to_smem.start()
    to_smem.wait()

    def row_copy(tok, k):
        return pltpu.make_async_copy(_row_tile(xn3_ref, tok), _row_tile(xs_out, dest_smem[k, tok]), sem_rows)

    def issue(tok, carry):
        for k in range(TOP_K):
            row_copy(tok, k).start()
        return carry

    lax.fori_loop(0, tm, issue, 0, unroll=8)

    def drain(tok, carry):
        for k in range(TOP_K):
            row_copy(tok, k).wait()
        return carry

    lax.fori_loop(0, tm, drain, 0, unroll=8)


def _dispatch(pstart, eidx, ranks, xn3, n_rows, tm=256):
    t = eidx.shape[1]
    meta = pl.BlockSpec((SUBLANES, tm), lambda i, ps: (0, i))
    any_spec = pl.BlockSpec(memory_space=pl.ANY)
    xs0 = jnp.zeros((n_rows * SUBLANES, LANES), F32)
    grid_spec = pltpu.PrefetchScalarGridSpec(
        num_scalar_prefetch=1,
        grid=(t // tm,),
        in_specs=[meta, meta, pl.BlockSpec((tm * SUBLANES, LANES), lambda i, ps: (i, 0)), any_spec],
        out_specs=[meta, any_spec],
        scratch_shapes=[pltpu.VMEM((SUBLANES, tm), I32), pltpu.SMEM((SUBLANES, tm), I32),
                        pltpu.SemaphoreType.DMA, pltpu.SemaphoreType.DMA],
    )
    return pl.pallas_call(
        _dispatch_kernel,
        grid_spec=grid_spec,
        out_shape=[jax.ShapeDtypeStruct((SUBLANES, t), I32), jax.ShapeDtypeStruct(xs0.shape, F32)],
        input_output_aliases={4: 1},
        compiler_params=_cparams(("arbitrary",)),
        name="dispatch",
    )(pstart, eidx, ranks, xn3, xs0)


def _experts_kernel(bexp_ref, nblk_ref, xs_ref, wup_ref, bup_ref, wdn_ref, bdn_ref, y_ref,
                    wup16, wdn16, x16):
    i = pl.program_id(0)
    rows = x16.shape[0]
    d = x16.shape[1]
    de = wdn16.shape[0]
    prev = bexp_ref[jnp.maximum(i - 1, 0)]
    fresh = (i == 0) | (bexp_ref[i] != prev)

    @pl.when(fresh)
    def _():
        wup16[...] = wup_ref[0].astype(BF16)
        wdn16[...] = wdn_ref[0].astype(BF16)

    @pl.when(i < nblk_ref[0])
    def _():
        for j in range(d // LANES):
            x16[:, j * LANES:(j + 1) * LANES] = xs_ref[pl.ds(j, rows, stride=SUBLANES), :].astype(BF16)
        hdn = jnp.dot(x16[...], wup16[...], preferred_element_type=F32) + bup_ref[0]
        g = jnp.minimum(hdn[:, :de], SWIGLU_LIMIT)
        u = jnp.clip(hdn[:, de:], -SWIGLU_LIMIT, SWIGLU_LIMIT)
        glu = g * jax.nn.sigmoid(SWIGLU_ALPHA * g)
        act = ((u + 1.0) * glu).astype(BF16)
        y = jnp.dot(act, wdn16[...], preferred_element_type=F32) + bdn_ref[0]
        for j in range(d // LANES):
            y_ref[pl.ds(j, rows, stride=SUBLANES), :] = y[:, j * LANES:(j + 1) * LANES]

    @pl.when(i >= nblk_ref[0])
    def _():
        y_ref[...] = jnp.zeros_like(y_ref)


def _experts(bexp, nblk, xs, w_up, b_up, w_down, b_down):
    n_exp, d, d2 = w_up.shape
    de = w_down.shape[1]
    n_blocks = bexp.shape[0]
    rows = ROW_BLOCK
    xspec = pl.BlockSpec((rows * SUBLANES, LANES), lambda i, be, nb: (i, 0))
    grid_spec = pltpu.PrefetchScalarGridSpec(
        num_scalar_prefetch=2,
        grid=(n_blocks,),
        in_specs=[xspec,
                  pl.BlockSpec((1, d, d2), lambda i, be, nb: (be[i], 0, 0)),
                  pl.BlockSpec((1, 1, d2), lambda i, be, nb: (be[i], 0, 0)),
                  pl.BlockSpec((1, de, d), lambda i, be, nb: (be[i], 0, 0)),
                  pl.BlockSpec((1, 1, d), lambda i, be, nb: (be[i], 0, 0))],
        out_specs=xspec,
        scratch_shapes=[pltpu.VMEM((d, d2), BF16), pltpu.VMEM((de, d), BF16), pltpu.VMEM((rows, d), BF16)],
    )
    return pl.pallas_call(
        _experts_kernel,
        grid_spec=grid_spec,
        out_shape=jax.ShapeDtypeStruct(xs.shape, F32),
        compiler_params=_cparams(("arbitrary",)),
        name="experts",
    )(bexp, nblk, xs, w_up, b_up[:, None, :], w_down, b_down[:, None, :])


def _combine_kernel(dest_ref, gate_ref, h_ref, p_ref, pg_ref, wpg_ref, wp_ref, fg_ref, y_hbm,
                    out_ref, dest_smem, ybuf, sem_idx, sem_rows):
    tm = h_ref.shape[0]
    d = h_ref.shape[1]
    to_smem = pltpu.make_async_copy(dest_ref, dest_smem, sem_idx)
    to_smem.start()
    to_smem.wait()

    def row_copy(tok, k):
        return pltpu.make_async_copy(_row_tile(y_hbm, dest_smem[k, tok]), _row_tile(ybuf.at[k], tok), sem_rows)

    def issue(tok, carry):
        for k in range(TOP_K):
            row_copy(tok, k).start()
        return carry

    lax.fori_loop(0, tm, issue, 0, unroll=8)

    def drain(tok, carry):
        for k in range(TOP_K):
            row_copy(tok, k).wait()
        return carry

    lax.fori_loop(0, tm, drain, 0, unroll=8)

    gates_t = jnp.transpose(gate_ref[...])
    pieces = []
    for j in range(d // LANES):
        acc = jnp.zeros((tm, LANES), F32)
        for k in range(TOP_K):
            acc = acc + gates_t[:, k:k + 1] * ybuf[k, pl.ds(j, tm, stride=SUBLANES), :]
        pieces.append(acc)
    h = h_ref[...] + jnp.concatenate(pieces, axis=1)

    ple = jnp.dot(p_ref[...].astype(BF16), wp_ref[...], preferred_element_type=F32)
    ple_gate = jax.nn.sigmoid(jnp.dot(_rms(h, pg_ref[...]).astype(BF16), wpg_ref[...],
                                      preferred_element_type=F32))
    h = h + ple * ple_gate
    out_ref[...] = _rms(h, fg_ref[...])


def _combine(dest, gates, h1, p2, pg, wpg, wp, fg, y, tm=256):
    t, d = h1.shape
    dp = p2.shape[1]
    row = lambda i: (i, 0)
    col = lambda i: (0, i)
    const = lambda i: (0, 0)
    return pl.pallas_call(
        _combine_kernel,
        grid=(t // tm,),
        in_specs=[pl.BlockSpec((SUBLANES, tm), col), pl.BlockSpec((SUBLANES, tm), col),
                  pl.BlockSpec((tm, d), row), pl.BlockSpec((tm, dp), row),
                  pl.BlockSpec((1, d), const), pl.BlockSpec((d, d), const), pl.BlockSpec((dp, d), const),
                  pl.BlockSpec((1, d), const), pl.BlockSpec(memory_space=pl.ANY)],
        out_specs=pl.BlockSpec((tm, d), row),
        out_shape=jax.ShapeDtypeStruct((t, d), F32),
        scratch_shapes=[pltpu.SMEM((SUBLANES, tm), I32), pltpu.VMEM((TOP_K, tm * SUBLANES, LANES), F32),
                        pltpu.SemaphoreType.DMA, pltpu.SemaphoreType.DMA],
        compiler_params=_cparams(("arbitrary",)),
        name="combine",
    )(dest, gates, h1, p2, pg, wpg, wp, fg, y)


def _block_diag(w):
    n, d, _ = w.shape
    eye = jnp.eye(n, dtype=w.dtype)
    return (eye[:, None, :, None] * w[:, :, None, :]).reshape(n * d, n * d)


def _stages(x, p, mix_norm_g, w_in, conv_w, conv_b, lru_w_a, lru_b_a, lru_w_x, lru_b_x, lru_lambda,
            lru_out_g, sb_out_g, w_out, ffn_norm_g, w_router, b_router, w_up, b_up, w_down, b_down,
            ple_norm_g, w_ple_gate, w_ple, final_norm_g):
    b, s, d = x.shape
    t = b * s
    st = {}
    x2 = x.reshape(t, d)
    lx, lg, q, k, v = _in_proj(x2, mix_norm_g[0][None], w_in[0].astype(BF16))
    st.update(lru_x=lx, lru_gate=lg, q=q, k=k, v=v)
    row = lambda a: a[None].astype(F32)
    lru_n = _lru(lx.reshape(b, s, D_LRU), lg.reshape(b, s, D_LRU), conv_w[0], row(conv_b[0]),
                 _block_diag(lru_w_a[0]).astype(BF16), row(lru_b_a[0]),
                 _block_diag(lru_w_x[0]).astype(BF16), row(lru_b_x[0]),
                 row(lru_lambda[0]), row(lru_out_g[0]))
    st["lru_n"] = lru_n
    sb_y = _sb_attn(q.reshape(b, s, D_SB), k.reshape(b, s, D_SB), v.reshape(b, s, D_SB))
    st["sb_y"] = sb_y
    wo = w_out[0].astype(BF16)
    h1, xn3, eidx, gates, ranks, counts = _out_route(
        x2, lru_n.reshape(t, D_LRU), sb_y.reshape(t, D_SB), row(sb_out_g[0]), wo[:D_LRU], wo[D_LRU:],
        row(ffn_norm_g[0]), w_router[0].T, b_router[0][:, None])
    st.update(h1=h1, xn1=xn3, eidx=eidx, gates=gates, ranks=ranks, counts=counts)

    cnt = counts[:, 0].astype(I32)
    padded = (cnt + ROW_BLOCK - 1) // ROW_BLOCK * ROW_BLOCK
    pend = jnp.cumsum(padded)
    pstart = pend - padded
    n_blocks = -(-(t * TOP_K) // ROW_BLOCK) + N_EXPERTS
    block_row0 = jnp.arange(n_blocks, dtype=I32) * ROW_BLOCK
    bexp = jnp.minimum(jnp.sum((pend[None, :] <= block_row0[:, None]).astype(I32), axis=1), N_EXPERTS - 1)
    nblk = (pend[-1:] // ROW_BLOCK).astype(I32)

    dest, xs = _dispatch(pstart, eidx, ranks, xn3, n_blocks * ROW_BLOCK)
    y = _experts(bexp, nblk, xs, w_up[0], b_up[0], w_down[0], b_down[0])
    st.update(dest=dest, xs=xs, y=y)
    out = _combine(dest, gates, h1, p[0].reshape(t, -1), row(ple_norm_g[0]), w_ple_gate[0].astype(BF16),
                   w_ple[0].astype(BF16), row(final_norm_g), y)
    st["final"] = out.reshape(b, s, d)
    return st


def kernel(x, p, mix_norm_g, w_in, conv_w, conv_b, lru_w_a, lru_b_a, lru_w_x, lru_b_x, lru_lambda,
           lru_out_g, sb_out_g, w_out, ffn_norm_g, w_router, b_router, w_up, b_up, w_down, b_down,
           ple_norm_g, w_ple_gate, w_ple, final_norm_g):
    return _stages(x, p, mix_norm_g, w_in, conv_w, conv_b, lru_w_a, lru_b_a, lru_w_x, lru_b_x, lru_lambda,
                   lru_out_g, sb_out_g, w_out, ffn_norm_g, w_router, b_router, w_up, b_up, w_down, b_down,
                   ple_norm_g, w_ple_gate, w_ple, final_norm_g)["final"]
```

```python
import functools

import jax
import jax.numpy as jnp
from jax import lax
from jax.experimental import pallas as pl
from jax.experimental.pallas import tpu as pltpu

F32 = jnp.float32
BF16 = jnp.bfloat16
I32 = jnp.int32

RMS_EPS = 1e-6
LANES = 128
SUBLANES = 8
D_LRU = 512
D_SB = 512
N_HEADS = 8
HEAD_DIM = 64
CONV_WIDTH = 4
LRU_C = 8.0
N_EXPERTS = 32
TOP_K = 4
SWIGLU_LIMIT = 7.0
SWIGLU_ALPHA = 1.702
ROW_BLOCK = 256
VMEM_LIMIT = 56 * 1024 * 1024


def _rms(x, g):
    return (x * lax.rsqrt(jnp.mean(x * x, axis=-1, keepdims=True) + RMS_EPS)) * g


def _cparams(sem, flags=None):
    return pltpu.CompilerParams(dimension_semantics=sem, vmem_limit_bytes=VMEM_LIMIT, flags=flags)


def _in_proj_kernel(x_ref, g_ref, w_ref, lx_ref, lg_ref, q_ref, k_ref, v_ref):
    xn = _rms(x_ref[...], g_ref[...])
    proj = jnp.dot(xn.astype(BF16), w_ref[...], preferred_element_type=F32)
    lx_ref[...] = proj[:, 0:D_LRU]
    lg_ref[...] = proj[:, D_LRU:2 * D_LRU]
    o = 2 * D_LRU
    q_ref[...] = proj[:, o:o + D_SB].astype(BF16)
    k_ref[...] = proj[:, o + D_SB:o + 2 * D_SB].astype(BF16)
    v_ref[...] = proj[:, o + 2 * D_SB:o + 3 * D_SB].astype(BF16)


def _in_proj(x2, g, w_bf, tm=512):
    t, d = x2.shape
    n = w_bf.shape[1]
    row = lambda i: (i, 0)
    const = lambda i: (0, 0)
    return pl.pallas_call(
        _in_proj_kernel,
        grid=(t // tm,),
        in_specs=[pl.BlockSpec((tm, d), row), pl.BlockSpec((1, d), const), pl.BlockSpec((d, n), const)],
        out_specs=[pl.BlockSpec((tm, D_LRU), row)] * 2 + [pl.BlockSpec((tm, D_SB), row)] * 3,
        out_shape=[jax.ShapeDtypeStruct((t, D_LRU), F32)] * 2 + [jax.ShapeDtypeStruct((t, D_SB), BF16)] * 3,
        compiler_params=_cparams(("parallel",)),
        name="in_proj",
    )(x2, g, w_bf)


def _shift_rows(x, k, fill):
    rolled = pltpu.roll(x, k, 0)
    rows = lax.broadcasted_iota(I32, x.shape, 0)
    return jnp.where(rows >= k, rolled, fill)


def _lru_kernel(lx_ref, lg_ref, cw_ref, cb_ref, wa_ref, ba_ref, wx_ref, bx_ref, lam_ref, og_ref,
                out_ref, tail_ref, h_ref):
    ts = lx_ref.shape[1]

    @pl.when(pl.program_id(1) == 0)
    def _():
        tail_ref[...] = jnp.zeros_like(tail_ref)
        h_ref[...] = jnp.zeros_like(h_ref)

    x = lx_ref[0]
    tail = tail_ref[...]
    rows = lax.broadcasted_iota(I32, x.shape, 0)
    cw = cw_ref[...]
    conv = x * cw[CONV_WIDTH - 1:CONV_WIDTH, :] + cb_ref[...]
    for k in range(1, CONV_WIDTH):
        cur = pltpu.roll(x, k, 0)
        prev = pltpu.roll(tail, k, 0)
        prev_full = jnp.concatenate([prev] + [prev] * (ts // SUBLANES - 1), axis=0)
        shifted = jnp.where(rows >= k, cur, prev_full)
        conv = conv + shifted * cw[CONV_WIDTH - 1 - k:CONV_WIDTH - k, :]
    tail_ref[...] = x[ts - SUBLANES:, :]

    cb16 = conv.astype(BF16)
    r = jax.nn.sigmoid(jnp.dot(cb16, wa_ref[...], preferred_element_type=F32) + ba_ref[...])
    gi = jax.nn.sigmoid(jnp.dot(cb16, wx_ref[...], preferred_element_type=F32) + bx_ref[...])
    lam = lam_ref[...]
    softplus_neg = jnp.maximum(-lam, 0.0) + jnp.log1p(jnp.exp(-jnp.abs(lam)))
    log_a = (-LRU_C * r) * softplus_neg
    a = jnp.exp(log_a)
    b = jnp.sqrt(1.0 - jnp.exp(2.0 * log_a)) * (gi * conv)

    k = 1
    while k < ts:
        a_sh = _shift_rows(a, k, 1.0)
        b_sh = _shift_rows(b, k, 0.0)
        b = a * b_sh + b
        a = a * a_sh
        k *= 2
    h = a * h_ref[0:1, :] + b
    h_ref[...] = jnp.broadcast_to(h[ts - 1:ts, :], h_ref.shape)

    gate = lg_ref[0]
    y = h * jax.nn.gelu(gate)
    out_ref[0] = _rms(y, og_ref[...]).astype(out_ref.dtype)


def _lru(lx, lg, cw, cb, wa, ba, wx, bx, lam, og, ts=256):
    b, s, c = lx.shape
    tile = lambda i, j: (i, j, 0)
    const = lambda i, j: (0, 0)
    vec = pl.BlockSpec((1, c), const)
    return pl.pallas_call(
        _lru_kernel,
        grid=(b, s // ts),
        in_specs=[pl.BlockSpec((1, ts, c), tile), pl.BlockSpec((1, ts, c), tile),
                  pl.BlockSpec((CONV_WIDTH, c), const), vec,
                  pl.BlockSpec((c, c), const), vec, pl.BlockSpec((c, c), const), vec, vec, vec],
        out_specs=pl.BlockSpec((1, ts, c), tile),
        out_shape=jax.ShapeDtypeStruct((b, s, c), BF16),
        scratch_shapes=[pltpu.VMEM((SUBLANES, c), F32), pltpu.VMEM((SUBLANES, c), F32)],
        compiler_params=_cparams(("parallel", "arbitrary")),
        name="lru",
    )(lx, lg, cw, cb, wa, ba, wx, bx, lam, og)


def _split_bf16(x):
    hi = x.astype(BF16)
    lo = (x - hi.astype(F32)).astype(BF16)
    return hi, lo


def _sb_attn_kernel(q_ref, k_ref, v_ref, o_ref, acc_ref, run_ref, *, tq, tk, slabs):
    qi = pl.program_id(2)
    lane = lax.broadcasted_iota(I32, (1, LANES), 1)
    head_masks = [lane < HEAD_DIM, lane >= HEAD_DIM]
    zero = jnp.zeros((), BF16)
    qs = []
    for sl in range(slabs):
        q = q_ref[0, :, sl * LANES:(sl + 1) * LANES] * jnp.asarray(HEAD_DIM ** -0.5, BF16)
        qs.append([jnp.where(m, q, zero) for m in head_masks])

    r = lax.broadcasted_iota(I32, (tk, tk), 0)
    c = lax.broadcasted_iota(I32, (tk, tk), 1)
    suffix = jnp.where(r > c, 1.0, 0.0).astype(BF16)

    acc_ref[...] = jnp.zeros_like(acc_ref)
    run_ref[...] = jnp.zeros_like(run_ref)

    def tile(j, diagonal):
        rows = pl.ds(pl.multiple_of(j * tk, tk), tk)
        if diagonal:
            qpos = lax.broadcasted_iota(I32, (tq, tk), 0)
            kpos = lax.broadcasted_iota(I32, (tq, tk), 1)
            visible = kpos < qpos
        for sl in range(slabs):
            ks = k_ref[0, rows, sl * LANES:(sl + 1) * LANES]
            vs = v_ref[0, rows, sl * LANES:(sl + 1) * LANES]
            pv = None
            for hd in range(2):
                z = lax.dot_general(qs[sl][hd], ks, (((1,), (1,)), ((), ())), preferred_element_type=F32)
                drop = jnp.maximum(z, 0.0) + jnp.log(1.0 + jnp.exp(-jnp.abs(z)))
                if diagonal:
                    drop = jnp.where(visible, drop, 0.0)
                sums = jnp.dot(drop.astype(BF16), suffix, preferred_element_type=F32)
                run = run_ref[2 * sl + hd]
                w = jnp.exp(z - ((drop + sums) + run))
                if diagonal:
                    w = jnp.where(visible, w, 0.0)
                vh = jnp.where(head_masks[hd], vs, zero)
                part = jnp.dot(w.astype(BF16), vh, preferred_element_type=F32)
                pv = part if pv is None else pv + part
                run_ref[2 * sl + hd] = run + (sums[:, 0:1] + drop[:, 0:1])
            acc_ref[:, sl * LANES:(sl + 1) * LANES] += pv

    tile(qi, True)

    def body(i, carry):
        tile(qi - 1 - i, False)
        return carry

    lax.fori_loop(0, qi, body, 0)
    o_ref[0] = acc_ref[...]


def _sb_attn(q, k, v, tq=512, slabs=2):
    b, s, c = q.shape
    tk = tq
    w = LANES * slabs
    qspec = pl.BlockSpec((1, tq, w), lambda i, j, l: (i, l, j))
    kvspec = pl.BlockSpec((1, s, w), lambda i, j, l: (i, 0, j))
    return pl.pallas_call(
        functools.partial(_sb_attn_kernel, tq=tq, tk=tk, slabs=slabs),
        grid=(b, c // w, s // tq),
        in_specs=[qspec, kvspec, kvspec],
        out_specs=qspec,
        out_shape=jax.ShapeDtypeStruct((b, s, c), F32),
        scratch_shapes=[pltpu.VMEM((tq, w), F32), pltpu.VMEM((2 * slabs, tq, 1), F32)],
        compiler_params=_cparams(("parallel", "parallel", "arbitrary")),
        name="sb_attn",
    )(q, k, v)


def _out_route_kernel(x_ref, lru_ref, sb_ref, sbg_ref, wol_ref, wos_ref, fg_ref, wr_ref, br_ref,
                      h_ref, xn3_ref, eidx_ref, gate_ref, rank_ref, cnt_ref, cnt_scr):
    tm = x_ref.shape[0]

    @pl.when(pl.program_id(0) == 0)
    def _():
        cnt_scr[...] = jnp.zeros_like(cnt_scr)

    sbn = _rms(sb_ref[...], sbg_ref[...]).astype(BF16)
    h = (x_ref[...] + jnp.dot(lru_ref[...], wol_ref[...], preferred_element_type=F32)
         + jnp.dot(sbn, wos_ref[...], preferred_element_type=F32))
    h_ref[...] = h
    xn = _rms(h, fg_ref[...])
    for j in range(xn.shape[1] // LANES):
        xn3_ref[pl.ds(j, tm, stride=SUBLANES), :] = xn[:, j * LANES:(j + 1) * LANES]

    nt = (((1,), (1,)), ((), ()))
    w_hi, w_lo = _split_bf16(wr_ref[...])
    x_hi, x_lo = _split_bf16(xn)
    logits = (lax.dot_general(w_hi, x_hi, nt, preferred_element_type=F32)
              + (lax.dot_general(w_hi, x_lo, nt, preferred_element_type=F32)
                 + lax.dot_general(w_lo, x_hi, nt, preferred_element_type=F32))) + br_ref[...]
    n_exp = logits.shape[0]
    eio = lax.broadcasted_iota(I32, logits.shape, 0)
    work = logits
    vals, hits, idxs = [], [], []
    for _k in range(TOP_K):
        m = jnp.max(work, axis=0, keepdims=True)
        idx = jnp.min(jnp.where(work == m, eio, n_exp), axis=0, keepdims=True)
        hit = eio == idx
        work = jnp.where(hit, -jnp.inf, work)
        vals.append(m)
        hits.append(hit)
        idxs.append(idx)
    exps = [jnp.exp(v - vals[0]) for v in vals]
    denom = exps[0] + exps[1] + exps[2] + exps[3]
    gates = [e / denom for e in exps]

    onehot = jnp.where(hits[0] | hits[1] | hits[2] | hits[3], 1.0, 0.0).astype(BF16)
    r = lax.broadcasted_iota(I32, (tm, tm + LANES), 0)
    c = lax.broadcasted_iota(I32, (tm, tm + LANES), 1)
    prefix_mat = jnp.where((r < c) | (c >= tm), 1.0, 0.0).astype(BF16)
    sums = jnp.dot(onehot, prefix_mat, preferred_element_type=F32)
    base = cnt_scr[...]
    pos = sums[:, :tm] + jnp.concatenate([base] * (tm // LANES), axis=1)
    ranks = [jnp.sum(jnp.where(hk, pos, 0.0), axis=0, keepdims=True) for hk in hits]
    cnt_scr[...] = base + sums[:, tm:]
    cnt_ref[...] = cnt_scr[...]

    pad_i = jnp.zeros((SUBLANES - TOP_K, tm), I32)
    eidx_ref[...] = jnp.concatenate(idxs + [pad_i], axis=0)
    rank_ref[...] = jnp.concatenate([rk.astype(I32) for rk in ranks] + [pad_i], axis=0)
    gate_ref[...] = jnp.concatenate(gates + [jnp.zeros((SUBLANES - TOP_K, tm), F32)], axis=0)


def _out_route(x2, lru_n, sb_y, sbg, wol, wos, fg, wr_t, br, tm=256):
    t, d = x2.shape
    e = wr_t.shape[0]
    row = lambda i: (i, 0)
    col = lambda i: (0, i)
    const = lambda i: (0, 0)
    meta = pl.BlockSpec((SUBLANES, tm), col)
    return pl.pallas_call(
        _out_route_kernel,
        grid=(t // tm,),
        in_specs=[pl.BlockSpec((tm, d), row), pl.BlockSpec((tm, D_LRU), row), pl.BlockSpec((tm, D_SB), row),
                  pl.BlockSpec((1, D_SB), const), pl.BlockSpec((D_LRU, d), const), pl.BlockSpec((D_SB, d), const),
                  pl.BlockSpec((1, d), const), pl.BlockSpec((e, d), const), pl.BlockSpec((e, 1), const)],
        out_specs=[pl.BlockSpec((tm, d), row), pl.BlockSpec((tm * SUBLANES, LANES), row), meta, meta, meta,
                   pl.BlockSpec((e, LANES), const)],
        out_shape=[jax.ShapeDtypeStruct((t, d), F32), jax.ShapeDtypeStruct((t * SUBLANES, LANES), F32),
                   jax.ShapeDtypeStruct((SUBLANES, t), I32), jax.ShapeDtypeStruct((SUBLANES, t), F32),
                   jax.ShapeDtypeStruct((SUBLANES, t), I32), jax.ShapeDtypeStruct((e, LANES), F32)],
        scratch_shapes=[pltpu.VMEM((e, LANES), F32)],
        compiler_params=_cparams(("arbitrary",)),
        name="out_route",
    )(x2, lru_n, sb_y, sbg, wol, wos, fg, wr_t, br)


def _row_tile(ref, row):
    return ref.at[pl.ds(pl.multiple_of(row * SUBLANES, SUBLANES), SUBLANES)]


def _dispatch_kernel(pstart_ref, cnt_ref, eidx_ref, rank_ref, xn3_ref, dest_ref, xs_out,
                     dest_vmem, dest_smem, zero_tile, sem_idx, sem_rows, sem_pad):
    tm = eidx_ref.shape[1]

    @pl.when(pl.program_id(0) == 0)
    def _():
        zero_tile[...] = jnp.zeros_like(zero_tile)

        def pad_copy(row):
            return pltpu.make_async_copy(zero_tile.at[pl.ds(0, SUBLANES)], _row_tile(xs_out, row), sem_pad)

        def per_expert(ex, carry):
            first_pad = pstart_ref[ex] + cnt_ref[ex]
            n_pad = (0 - cnt_ref[ex]) & (ROW_BLOCK - 1)

            def fill(r, c):
                pad_copy(first_pad + r).start()
                return c

            def drain(r, c):
                pad_copy(first_pad + r).wait()
                return c

            lax.fori_loop(0, n_pad, fill, 0)
            lax.fori_loop(0, n_pad, drain, 0)
            return carry

        lax.fori_loop(0, N_EXPERTS, per_expert, 0)

        last = N_EXPERTS - 1
        used_rows = pstart_ref[last] + cnt_ref[last] + ((0 - cnt_ref[last]) & (ROW_BLOCK - 1))
        block_rows = ROW_BLOCK * SUBLANES

        def block_copy(blk):
            return pltpu.make_async_copy(
                zero_tile, xs_out.at[pl.ds(pl.multiple_of(blk * block_rows, block_rows), block_rows)], sem_pad)

        def fill_block(blk, c):
            block_copy(blk).start()
            return c

        def drain_block(blk, c):
            block_copy(blk).wait()
            return c

        first_free = used_rows // ROW_BLOCK
        lax.fori_loop(first_free, xs_out.shape[0] // block_rows, fill_block, 0)
        lax.fori_loop(first_free, xs_out.shape[0] // block_rows, drain_block, 0)

    e = eidx_ref[...]
    start = jnp.zeros_like(e)
    for ex in range(N_EXPERTS):
        start = jnp.where(e == ex, pstart_ref[ex], start)
    dest = rank_ref[...] + start
    dest_ref[...] = dest
    dest_vmem[...] = dest
    to_smem = pltpu.make_async_copy(dest_vmem, dest_smem, sem_idx)
    to_smem.start()
    to_smem.wait()

    def row_copy(tok, k):
        return pltpu.make_async_copy(_row_tile(xn3_ref, tok), _row_tile(xs_out, dest_smem[k, tok]), sem_rows)

    def issue(tok, carry):
        for k in range(TOP_K):
            row_copy(tok, k).start()
        return carry

    lax.fori_loop(0, tm, issue, 0, unroll=8)

    def drain_rows(tok, carry):
        for k in range(TOP_K):
            row_copy(tok, k).wait()
        return carry

    lax.fori_loop(0, tm, drain_rows, 0, unroll=8)


def _dispatch(pstart, cnt, eidx, ranks, xn3, n_rows, tm=256):
    t = eidx.shape[1]
    meta = pl.BlockSpec((SUBLANES, tm), lambda i, ps, cn: (0, i))
    any_spec = pl.BlockSpec(memory_space=pl.ANY)
    grid_spec = pltpu.PrefetchScalarGridSpec(
        num_scalar_prefetch=2,
        grid=(t // tm,),
        in_specs=[meta, meta, pl.BlockSpec((tm * SUBLANES, LANES), lambda i, ps, cn: (i, 0))],
        out_specs=[meta, any_spec],
        scratch_shapes=[pltpu.VMEM((SUBLANES, tm), I32), pltpu.SMEM((SUBLANES, tm), I32),
                        pltpu.VMEM((ROW_BLOCK * SUBLANES, LANES), F32),
                        pltpu.SemaphoreType.DMA, pltpu.SemaphoreType.DMA, pltpu.SemaphoreType.DMA],
    )
    return pl.pallas_call(
        _dispatch_kernel,
        grid_spec=grid_spec,
        out_shape=[jax.ShapeDtypeStruct((SUBLANES, t), I32),
                   jax.ShapeDtypeStruct((n_rows * SUBLANES, LANES), F32)],
        compiler_params=_cparams(("arbitrary",)),
        name="dispatch",
    )(pstart, cnt, eidx, ranks, xn3)


def _experts_kernel(bexp_ref, nblk_ref, nbe_ref, xs_ref, bup_ref, bdn_ref, wup_hbm, wdn_hbm, y_ref,
                    wup32, wdn32, wup16, wdn16, x16, sem_up, sem_dn):
    i = pl.program_id(0)
    rows = x16.shape[0]
    d = x16.shape[1]
    de = wdn16.shape[0]
    e = bexp_ref[i]
    active = i < nblk_ref[0]
    first = (i == 0) | (e != bexp_ref[jnp.maximum(i - 1, 0)])

    def fetch(ex):
        return (pltpu.make_async_copy(wup_hbm.at[ex], wup32, sem_up),
                pltpu.make_async_copy(wdn_hbm.at[ex], wdn32, sem_dn))

    @pl.when(i == 0)
    def _():
        for cp in fetch(e):
            cp.start()

    @pl.when(active & first)
    def _():
        for cp in fetch(e):
            cp.wait()
        wup16[...] = wup32[...].astype(BF16)
        wdn16[...] = wdn32[...].astype(BF16)
        nxt = i + nbe_ref[e]

        @pl.when(nxt < nblk_ref[0])
        def _():
            for cp in fetch(bexp_ref[nxt]):
                cp.start()

    @pl.when(active)
    def _():
        for j in range(d // LANES):
            x16[:, j * LANES:(j + 1) * LANES] = xs_ref[pl.ds(j, rows, stride=SUBLANES), :].astype(BF16)
        hdn = jnp.dot(x16[...], wup16[...], preferred_element_type=F32) + bup_ref[0]
        g = jnp.minimum(hdn[:, :de], SWIGLU_LIMIT)
        u = jnp.clip(hdn[:, de:], -SWIGLU_LIMIT, SWIGLU_LIMIT)
        glu = g * jax.nn.sigmoid(SWIGLU_ALPHA * g)
        act = ((u + 1.0) * glu).astype(BF16)
        y = jnp.dot(act, wdn16[...], preferred_element_type=F32) + bdn_ref[0]
        for j in range(d // LANES):
            y_ref[pl.ds(j, rows, stride=SUBLANES), :] = y[:, j * LANES:(j + 1) * LANES]

    @pl.when(jnp.logical_not(active))
    def _():
        y_ref[...] = jnp.zeros_like(y_ref)


def _experts(bexp, nblk, nbe, xs, w_up, b_up, w_down, b_down):
    n_exp, d, d2 = w_up.shape
    de = w_down.shape[1]
    n_blocks = bexp.shape[0]
    rows = ROW_BLOCK
    xspec = pl.BlockSpec((rows * SUBLANES, LANES), lambda i, be, nb, ne: (i, 0))
    any_spec = pl.BlockSpec(memory_space=pl.ANY)
    grid_spec = pltpu.PrefetchScalarGridSpec(
        num_scalar_prefetch=3,
        grid=(n_blocks,),
        in_specs=[xspec,
                  pl.BlockSpec((1, 1, d2), lambda i, be, nb, ne: (be[i], 0, 0)),
                  pl.BlockSpec((1, 1, d), lambda i, be, nb, ne: (be[i], 0, 0)),
                  any_spec, any_spec],
        out_specs=xspec,
        scratch_shapes=[pltpu.VMEM((d, d2), F32), pltpu.VMEM((de, d), F32),
                        pltpu.VMEM((d, d2), BF16), pltpu.VMEM((de, d), BF16), pltpu.VMEM((rows, d), BF16),
                        pltpu.SemaphoreType.DMA, pltpu.SemaphoreType.DMA],
    )
    return pl.pallas_call(
        _experts_kernel,
        grid_spec=grid_spec,
        out_shape=jax.ShapeDtypeStruct(xs.shape, F32),
        compiler_params=_cparams(("arbitrary",)),
        name="experts",
    )(bexp, nblk, nbe, xs, b_up[:, None, :], b_down[:, None, :], w_up, w_down)


def _combine_kernel(dest_ref, dest_next_ref, gate_ref, h_ref, p_ref, pg_ref, wpg_ref, wp_ref, fg_ref, y_hbm,
                    out_ref, dest_smem, ybuf, sem_idx, sem_rows):
    tm = h_ref.shape[0]
    d = h_ref.shape[1]
    i = pl.program_id(0)
    n = pl.num_programs(0)
    slot = lax.rem(i, 2)

    def row_copy(tok, k, sl):
        return pltpu.make_async_copy(_row_tile(y_hbm, dest_smem[k, tok]), _row_tile(ybuf.at[sl, k], tok),
                                     sem_rows.at[sl])

    def gather(idx_ref, sl):
        to_smem = pltpu.make_async_copy(idx_ref, dest_smem, sem_idx)
        to_smem.start()
        to_smem.wait()

        def issue(tok, carry):
            for k in range(TOP_K):
                row_copy(tok, k, sl).start()
            return carry

        lax.fori_loop(0, tm, issue, 0, unroll=8)

    @pl.when(i == 0)
    def _():
        gather(dest_ref, 0)

    @pl.when(i + 1 < n)
    def _():
        gather(dest_next_ref, 1 - slot)

    def drain(tok, carry):
        for k in range(TOP_K):
            pltpu.make_async_copy(_row_tile(y_hbm, 0), _row_tile(ybuf.at[slot, k], tok), sem_rows.at[slot]).wait()
        return carry

    lax.fori_loop(0, tm, drain, 0, unroll=8)

    gates_t = jnp.transpose(gate_ref[...])
    pieces = []
    for j in range(d // LANES):
        acc = jnp.zeros((tm, LANES), F32)
        for k in range(TOP_K):
            acc = acc + gates_t[:, k:k + 1] * ybuf[slot, k, pl.ds(j, tm, stride=SUBLANES), :]
        pieces.append(acc)
    h = h_ref[...] + jnp.concatenate(pieces, axis=1)

    ple = jnp.dot(p_ref[...].astype(BF16), wp_ref[...], preferred_element_type=F32)
    ple_gate = jax.nn.sigmoid(jnp.dot(_rms(h, pg_ref[...]).astype(BF16), wpg_ref[...],
                                      preferred_element_type=F32))
    h = h + ple * ple_gate
    out_ref[...] = _rms(h, fg_ref[...])


def _combine(dest, gates, h1, p2, pg, wpg, wp, fg, y, tm=256):
    t, d = h1.shape
    dp = p2.shape[1]
    row = lambda i: (i, 0)
    col = lambda i: (0, i)
    const = lambda i: (0, 0)
    return pl.pallas_call(
        _combine_kernel,
        grid=(t // tm,),
        in_specs=[pl.BlockSpec((SUBLANES, tm), col),
                  pl.BlockSpec((SUBLANES, tm), lambda i: (0, jnp.minimum(i + 1, t // tm - 1))),
                  pl.BlockSpec((SUBLANES, tm), col),
                  pl.BlockSpec((tm, d), row), pl.BlockSpec((tm, dp), row),
                  pl.BlockSpec((1, d), const), pl.BlockSpec((d, d), const), pl.BlockSpec((dp, d), const),
                  pl.BlockSpec((1, d), const), pl.BlockSpec(memory_space=pl.ANY)],
        out_specs=pl.BlockSpec((tm, d), row),
        out_shape=jax.ShapeDtypeStruct((t, d), F32),
        scratch_shapes=[pltpu.SMEM((SUBLANES, tm), I32), pltpu.VMEM((2, TOP_K, tm * SUBLANES, LANES), F32),
                        pltpu.SemaphoreType.DMA, pltpu.SemaphoreType.DMA((2,))],
        compiler_params=_cparams(("arbitrary",)),
        name="combine",
    )(dest, dest, gates, h1, p2, pg, wpg, wp, fg, y)


def _block_diag(w):
    n, d, _ = w.shape
    eye = jnp.eye(n, dtype=w.dtype)
    return (eye[:, None, :, None] * w[:, :, None, :]).reshape(n * d, n * d)


def _stages(x, p, mix_norm_g, w_in, conv_w, conv_b, lru_w_a, lru_b_a, lru_w_x, lru_b_x, lru_lambda,
            lru_out_g, sb_out_g, w_out, ffn_norm_g, w_router, b_router, w_up, b_up, w_down, b_down,
            ple_norm_g, w_ple_gate, w_ple, final_norm_g):
    b, s, d = x.shape
    t = b * s
    st = {}
    x2 = x.reshape(t, d)
    lx, lg, q, k, v = _in_proj(x2, mix_norm_g[0][None], w_in[0].astype(BF16))
    st.update(lru_x=lx, lru_gate=lg, q=q, k=k, v=v)
    row = lambda a: a[None].astype(F32)
    lru_n = _lru(lx.reshape(b, s, D_LRU), lg.reshape(b, s, D_LRU), conv_w[0], row(conv_b[0]),
                 _block_diag(lru_w_a[0]).astype(BF16), row(lru_b_a[0]),
                 _block_diag(lru_w_x[0]).astype(BF16), row(lru_b_x[0]),
                 row(lru_lambda[0]), row(lru_out_g[0]))
    st["lru_n"] = lru_n
    sb_y = _sb_attn(q.reshape(b, s, D_SB), k.reshape(b, s, D_SB), v.reshape(b, s, D_SB))
    st["sb_y"] = sb_y
    wo = w_out[0].astype(BF16)
    h1, xn3, eidx, gates, ranks, counts = _out_route(
        x2, lru_n.reshape(t, D_LRU), sb_y.reshape(t, D_SB), row(sb_out_g[0]), wo[:D_LRU], wo[D_LRU:],
        row(ffn_norm_g[0]), w_router[0].T, b_router[0][:, None])
    st.update(h1=h1, xn1=xn3, eidx=eidx, gates=gates, ranks=ranks, counts=counts)

    cnt = counts[:, 0].astype(I32)
    padded = (cnt + ROW_BLOCK - 1) // ROW_BLOCK * ROW_BLOCK
    pend = jnp.cumsum(padded)
    pstart = pend - padded
    n_blocks = -(-(t * TOP_K) // ROW_BLOCK) + N_EXPERTS
    block_row0 = jnp.arange(n_blocks, dtype=I32) * ROW_BLOCK
    bexp = jnp.minimum(jnp.sum((pend[None, :] <= block_row0[:, None]).astype(I32), axis=1), N_EXPERTS - 1)
    nblk = (pend[-1:] // ROW_BLOCK).astype(I32)

    dest, xs = _dispatch(pstart, cnt, eidx, ranks, xn3, n_blocks * ROW_BLOCK)
    y = _experts(bexp, nblk, padded // ROW_BLOCK, xs, w_up[0], b_up[0], w_down[0], b_down[0])
    st.update(dest=dest, xs=xs, y=y)
    out = _combine(dest, gates, h1, p[0].reshape(t, -1), row(ple_norm_g[0]), w_ple_gate[0].astype(BF16),
                   w_ple[0].astype(BF16), row(final_norm_g), y)
    st["final"] = out.reshape(b, s, d)
    return st


def kernel(x, p, mix_norm_g, w_in, conv_w, conv_b, lru_w_a, lru_b_a, lru_w_x, lru_b_x, lru_lambda,
           lru_out_g, sb_out_g, w_out, ffn_norm_g, w_router, b_router, w_up, b_up, w_down, b_down,
           ple_norm_g, w_ple_gate, w_ple, final_norm_g):
    return _stages(x, p, mix_norm_g, w_in, conv_w, conv_b, lru_w_a, lru_b_a, lru_w_x, lru_b_x, lru_lambda,
                   lru_out_g, sb_out_g, w_out, ffn_norm_g, w_router, b_router, w_up, b_up, w_down, b_down,
                   ple_norm_g, w_ple_gate, w_ple, final_norm_g)["final"]
```

```python
import functools

import jax
import jax.numpy as jnp
from jax import lax
from jax.experimental import pallas as pl
from jax.experimental.pallas import tpu as pltpu

F32 = jnp.float32
BF16 = jnp.bfloat16
I32 = jnp.int32

RMS_EPS = 1e-6
LANES = 128
SUBLANES = 8
D_LRU = 512
D_SB = 512
N_HEADS = 8
HEAD_DIM = 64
CONV_WIDTH = 4
LRU_C = 8.0
N_EXPERTS = 32
TOP_K = 4
SWIGLU_LIMIT = 7.0
SWIGLU_ALPHA = 1.702
ROW_BLOCK = 256
VMEM_LIMIT = 56 * 1024 * 1024


def _rms(x, g):
    return (x * lax.rsqrt(jnp.mean(x * x, axis=-1, keepdims=True) + RMS_EPS)) * g


def _cparams(sem, flags=None):
    return pltpu.CompilerParams(dimension_semantics=sem, vmem_limit_bytes=VMEM_LIMIT, flags=flags)


def _in_proj_kernel(x_ref, g_ref, w_ref, lx_ref, lg_ref, q_ref, k_ref, v_ref):
    xn = _rms(x_ref[...], g_ref[...])
    proj = jnp.dot(xn.astype(BF16), w_ref[...], preferred_element_type=F32)
    lx_ref[...] = proj[:, 0:D_LRU]
    lg_ref[...] = proj[:, D_LRU:2 * D_LRU]
    o = 2 * D_LRU
    q_ref[...] = proj[:, o:o + D_SB].astype(BF16)
    k_ref[...] = proj[:, o + D_SB:o + 2 * D_SB].astype(BF16)
    v_ref[...] = proj[:, o + 2 * D_SB:o + 3 * D_SB].astype(BF16)


def _in_proj(x2, g, w_bf, tm=512):
    t, d = x2.shape
    n = w_bf.shape[1]
    row = lambda i: (i, 0)
    const = lambda i: (0, 0)
    return pl.pallas_call(
        _in_proj_kernel,
        grid=(t // tm,),
        in_specs=[pl.BlockSpec((tm, d), row), pl.BlockSpec((1, d), const), pl.BlockSpec((d, n), const)],
        out_specs=[pl.BlockSpec((tm, D_LRU), row)] * 2 + [pl.BlockSpec((tm, D_SB), row)] * 3,
        out_shape=[jax.ShapeDtypeStruct((t, D_LRU), F32)] * 2 + [jax.ShapeDtypeStruct((t, D_SB), BF16)] * 3,
        compiler_params=_cparams(("parallel",)),
        name="in_proj",
    )(x2, g, w_bf)


def _shift_rows(x, k, fill):
    if k % SUBLANES == 0:
        return jnp.concatenate([jnp.full((k, x.shape[1]), fill, x.dtype), x[:x.shape[0] - k]], axis=0)
    rolled = pltpu.roll(x, k, 0)
    rows = lax.broadcasted_iota(I32, x.shape, 0)
    return jnp.where(rows >= k, rolled, fill)


def _lru_kernel(lx_ref, lg_ref, cw_ref, cb_ref, wa_ref, ba_ref, wx_ref, bx_ref, lam_ref, og_ref,
                out_ref, tail_ref, h_ref):
    ts = lx_ref.shape[1]

    @pl.when(pl.program_id(1) == 0)
    def _():
        tail_ref[...] = jnp.zeros_like(tail_ref)
        h_ref[...] = jnp.zeros_like(h_ref)

    x = lx_ref[0]
    tail = tail_ref[...]
    rows = lax.broadcasted_iota(I32, x.shape, 0)
    cw = cw_ref[...]
    conv = x * cw[CONV_WIDTH - 1:CONV_WIDTH, :] + cb_ref[...]
    for k in range(1, CONV_WIDTH):
        cur = pltpu.roll(x, k, 0)
        prev = pltpu.roll(tail, k, 0)
        prev_full = jnp.concatenate([prev] + [prev] * (ts // SUBLANES - 1), axis=0)
        shifted = jnp.where(rows >= k, cur, prev_full)
        conv = conv + shifted * cw[CONV_WIDTH - 1 - k:CONV_WIDTH - k, :]
    tail_ref[...] = x[ts - SUBLANES:, :]

    cb16 = conv.astype(BF16)
    r = jax.nn.sigmoid(jnp.dot(cb16, wa_ref[...], preferred_element_type=F32) + ba_ref[...])
    gi = jax.nn.sigmoid(jnp.dot(cb16, wx_ref[...], preferred_element_type=F32) + bx_ref[...])
    lam = lam_ref[...]
    softplus_neg = jnp.maximum(-lam, 0.0) + jnp.log1p(jnp.exp(-jnp.abs(lam)))
    log_a = (-LRU_C * r) * softplus_neg
    a = jnp.exp(log_a)
    b = jnp.sqrt(1.0 - jnp.exp(2.0 * log_a)) * (gi * conv)

    k = 1
    while k < ts:
        a_sh = _shift_rows(a, k, 1.0)
        b_sh = _shift_rows(b, k, 0.0)
        b = a * b_sh + b
        a = a * a_sh
        k *= 2
    h = a * h_ref[0:1, :] + b
    h_ref[...] = jnp.broadcast_to(h[ts - 1:ts, :], h_ref.shape)

    gate = lg_ref[0]
    y = h * jax.nn.gelu(gate)
    out_ref[0] = _rms(y, og_ref[...]).astype(out_ref.dtype)


def _lru(lx, lg, cw, cb, wa, ba, wx, bx, lam, og, ts=256):
    b, s, c = lx.shape
    tile = lambda i, j: (i, j, 0)
    const = lambda i, j: (0, 0)
    vec = pl.BlockSpec((1, c), const)
    return pl.pallas_call(
        _lru_kernel,
        grid=(b, s // ts),
        in_specs=[pl.BlockSpec((1, ts, c), tile), pl.BlockSpec((1, ts, c), tile),
                  pl.BlockSpec((CONV_WIDTH, c), const), vec,
                  pl.BlockSpec((c, c), const), vec, pl.BlockSpec((c, c), const), vec, vec, vec],
        out_specs=pl.BlockSpec((1, ts, c), tile),
        out_shape=jax.ShapeDtypeStruct((b, s, c), BF16),
        scratch_shapes=[pltpu.VMEM((SUBLANES, c), F32), pltpu.VMEM((SUBLANES, c), F32)],
        compiler_params=_cparams(("parallel", "arbitrary")),
        name="lru",
    )(lx, lg, cw, cb, wa, ba, wx, bx, lam, og)


def _split_bf16(x):
    hi = x.astype(BF16)
    lo = (x - hi.astype(F32)).astype(BF16)
    return hi, lo


def _sb_attn_kernel(q_ref, k_ref, v_ref, o_ref, acc_ref, run_ref, *, tq, tk, slabs):
    qi = pl.program_id(2)
    lane = lax.broadcasted_iota(I32, (1, LANES), 1)
    head_masks = [lane < HEAD_DIM, lane >= HEAD_DIM]
    zero = jnp.zeros((), BF16)
    qs = []
    for sl in range(slabs):
        q = q_ref[0, :, sl * LANES:(sl + 1) * LANES] * jnp.asarray(HEAD_DIM ** -0.5, BF16)
        qs.append([jnp.where(m, q, zero) for m in head_masks])

    r = lax.broadcasted_iota(I32, (tk, tk), 0)
    c = lax.broadcasted_iota(I32, (tk, tk), 1)
    suffix = jnp.where(r > c, 1.0, 0.0).astype(BF16)

    acc_ref[...] = jnp.zeros_like(acc_ref)
    run_ref[...] = jnp.zeros_like(run_ref)

    def tile(j, diagonal):
        rows = pl.ds(pl.multiple_of(j * tk, tk), tk)
        if diagonal:
            qpos = lax.broadcasted_iota(I32, (tq, tk), 0)
            kpos = lax.broadcasted_iota(I32, (tq, tk), 1)
            visible = kpos < qpos
        for sl in range(slabs):
            ks = k_ref[0, rows, sl * LANES:(sl + 1) * LANES]
            vs = v_ref[0, rows, sl * LANES:(sl + 1) * LANES]
            pv = None
            for hd in range(2):
                z = lax.dot_general(qs[sl][hd], ks, (((1,), (1,)), ((), ())), preferred_element_type=F32)
                drop = jnp.maximum(z, 0.0) + jnp.log(1.0 + jnp.exp(-jnp.abs(z)))
                if diagonal:
                    drop = jnp.where(visible, drop, 0.0)
                sums = jnp.dot(drop.astype(BF16), suffix, preferred_element_type=F32)
                run = run_ref[2 * sl + hd]
                w = jnp.exp(z - ((drop + sums) + run))
                if diagonal:
                    w = jnp.where(visible, w, 0.0)
                vh = jnp.where(head_masks[hd], vs, zero)
                part = jnp.dot(w.astype(BF16), vh, preferred_element_type=F32)
                pv = part if pv is None else pv + part
                run_ref[2 * sl + hd] = run + (sums[:, 0:1] + drop[:, 0:1])
            acc_ref[:, sl * LANES:(sl + 1) * LANES] += pv

    tile(qi, True)

    def body(i, carry):
        tile(qi - 1 - i, False)
        return carry

    lax.fori_loop(0, qi, body, 0)
    o_ref[0] = acc_ref[...]


def _sb_attn(q, k, v, tq=512, slabs=2):
    b, s, c = q.shape
    tk = tq
    w = LANES * slabs
    qspec = pl.BlockSpec((1, tq, w), lambda i, j, l: (i, l, j))
    kvspec = pl.BlockSpec((1, s, w), lambda i, j, l: (i, 0, j))
    return pl.pallas_call(
        functools.partial(_sb_attn_kernel, tq=tq, tk=tk, slabs=slabs),
        grid=(b, c // w, s // tq),
        in_specs=[qspec, kvspec, kvspec],
        out_specs=qspec,
        out_shape=jax.ShapeDtypeStruct((b, s, c), F32),
        scratch_shapes=[pltpu.VMEM((tq, w), F32), pltpu.VMEM((2 * slabs, tq, 1), F32)],
        compiler_params=_cparams(("parallel", "parallel", "arbitrary")),
        name="sb_attn",
    )(q, k, v)


def _out_route_kernel(x_ref, lru_ref, sb_ref, sbg_ref, wol_ref, wos_ref, fg_ref, wr_ref, br_ref,
                      h_ref, xn3_ref, eidx_ref, gate_ref, rank_ref, cnt_ref, cnt_scr):
    tm = x_ref.shape[0]

    @pl.when(pl.program_id(0) == 0)
    def _():
        cnt_scr[...] = jnp.zeros_like(cnt_scr)

    sbn = _rms(sb_ref[...], sbg_ref[...]).astype(BF16)
    h = (x_ref[...] + jnp.dot(lru_ref[...], wol_ref[...], preferred_element_type=F32)
         + jnp.dot(sbn, wos_ref[...], preferred_element_type=F32))
    h_ref[...] = h
    xn = _rms(h, fg_ref[...])
    for j in range(xn.shape[1] // LANES):
        xn3_ref[pl.ds(j, tm, stride=SUBLANES), :] = xn[:, j * LANES:(j + 1) * LANES]

    nt = (((1,), (1,)), ((), ()))
    w_hi, w_lo = _split_bf16(wr_ref[...])
    x_hi, x_lo = _split_bf16(xn)
    logits = (lax.dot_general(w_hi, x_hi, nt, preferred_element_type=F32)
              + (lax.dot_general(w_hi, x_lo, nt, preferred_element_type=F32)
                 + lax.dot_general(w_lo, x_hi, nt, preferred_element_type=F32))) + br_ref[...]
    n_exp = logits.shape[0]
    eio = lax.broadcasted_iota(I32, logits.shape, 0)
    work = logits
    vals, hits, idxs = [], [], []
    for _k in range(TOP_K):
        m = jnp.max(work, axis=0, keepdims=True)
        idx = jnp.min(jnp.where(work == m, eio, n_exp), axis=0, keepdims=True)
        hit = eio == idx
        work = jnp.where(hit, -jnp.inf, work)
        vals.append(m)
        hits.append(hit)
        idxs.append(idx)
    exps = [jnp.exp(v - vals[0]) for v in vals]
    denom = exps[0] + exps[1] + exps[2] + exps[3]
    gates = [e / denom for e in exps]

    onehot = jnp.where(hits[0] | hits[1] | hits[2] | hits[3], 1.0, 0.0).astype(BF16)
    r = lax.broadcasted_iota(I32, (tm, tm + LANES), 0)
    c = lax.broadcasted_iota(I32, (tm, tm + LANES), 1)
    prefix_mat = jnp.where((r < c) | (c >= tm), 1.0, 0.0).astype(BF16)
    sums = jnp.dot(onehot, prefix_mat, preferred_element_type=F32)
    base = cnt_scr[...]
    pos = sums[:, :tm] + jnp.concatenate([base] * (tm // LANES), axis=1)
    ranks = [jnp.sum(jnp.where(hk, pos, 0.0), axis=0, keepdims=True) for hk in hits]
    cnt_scr[...] = base + sums[:, tm:]
    cnt_ref[...] = cnt_scr[...]

    pad_i = jnp.zeros((SUBLANES - TOP_K, tm), I32)
    eidx_ref[...] = jnp.concatenate(idxs + [pad_i], axis=0)
    rank_ref[...] = jnp.concatenate([rk.astype(I32) for rk in ranks] + [pad_i], axis=0)
    gate_ref[...] = jnp.concatenate(gates + [jnp.zeros((SUBLANES - TOP_K, tm), F32)], axis=0)


def _out_route(x2, lru_n, sb_y, sbg, wol, wos, fg, wr_t, br, tm=512):
    t, d = x2.shape
    e = wr_t.shape[0]
    row = lambda i: (i, 0)
    col = lambda i: (0, i)
    const = lambda i: (0, 0)
    meta = pl.BlockSpec((SUBLANES, tm), col)
    return pl.pallas_call(
        _out_route_kernel,
        grid=(t // tm,),
        in_specs=[pl.BlockSpec((tm, d), row), pl.BlockSpec((tm, D_LRU), row), pl.BlockSpec((tm, D_SB), row),
                  pl.BlockSpec((1, D_SB), const), pl.BlockSpec((D_LRU, d), const), pl.BlockSpec((D_SB, d), const),
                  pl.BlockSpec((1, d), const), pl.BlockSpec((e, d), const), pl.BlockSpec((e, 1), const)],
        out_specs=[pl.BlockSpec((tm, d), row), pl.BlockSpec((tm * SUBLANES, LANES), row), meta, meta, meta,
                   pl.BlockSpec((e, LANES), const)],
        out_shape=[jax.ShapeDtypeStruct((t, d), F32), jax.ShapeDtypeStruct((t * SUBLANES, LANES), F32),
                   jax.ShapeDtypeStruct((SUBLANES, t), I32), jax.ShapeDtypeStruct((SUBLANES, t), F32),
                   jax.ShapeDtypeStruct((SUBLANES, t), I32), jax.ShapeDtypeStruct((e, LANES), F32)],
        scratch_shapes=[pltpu.VMEM((e, LANES), F32)],
        compiler_params=_cparams(("arbitrary",)),
        name="out_route",
    )(x2, lru_n, sb_y, sbg, wol, wos, fg, wr_t, br)


def _row_tile(ref, row):
    return ref.at[pl.ds(pl.multiple_of(row * SUBLANES, SUBLANES), SUBLANES)]


def _dispatch_kernel(pstart_ref, cnt_ref, eidx_ref, rank_ref, xn3_ref, dest_ref, xs_out,
                     dest_vmem, dest_smem, zero_tile, sem_idx, sem_rows, sem_pad, sem_blk):
    tm = eidx_ref.shape[1]

    @pl.when(pl.program_id(0) == 0)
    def _():
        zero_tile[...] = jnp.zeros_like(zero_tile)

        def pad_copy(row):
            return pltpu.make_async_copy(zero_tile.at[pl.ds(0, SUBLANES)], _row_tile(xs_out, row), sem_pad)

        def pads_of(ex, action):
            first_pad = pstart_ref[ex] + cnt_ref[ex]
            n_pad = (0 - cnt_ref[ex]) & (ROW_BLOCK - 1)

            def one(r, c):
                action(pad_copy(first_pad + r))
                return c

            lax.fori_loop(0, n_pad, one, 0)

        def fill_expert(ex, carry):
            pads_of(ex, lambda cp: cp.start())
            return carry

        def drain_expert(ex, carry):
            pads_of(ex, lambda cp: cp.wait())
            return carry

        lax.fori_loop(0, N_EXPERTS, fill_expert, 0)

        last = N_EXPERTS - 1
        used_rows = pstart_ref[last] + cnt_ref[last] + ((0 - cnt_ref[last]) & (ROW_BLOCK - 1))
        block_rows = ROW_BLOCK * SUBLANES

        def block_copy(blk):
            return pltpu.make_async_copy(
                zero_tile, xs_out.at[pl.ds(pl.multiple_of(blk * block_rows, block_rows), block_rows)], sem_blk)

        def fill_block(blk, c):
            block_copy(blk).start()
            return c

        def drain_block(blk, c):
            block_copy(blk).wait()
            return c

        first_free = used_rows // ROW_BLOCK
        lax.fori_loop(first_free, xs_out.shape[0] // block_rows, fill_block, 0)
        lax.fori_loop(0, N_EXPERTS, drain_expert, 0)
        lax.fori_loop(first_free, xs_out.shape[0] // block_rows, drain_block, 0)

    e = eidx_ref[...]
    start = jnp.zeros_like(e)
    for ex in range(N_EXPERTS):
        start = jnp.where(e == ex, pstart_ref[ex], start)
    dest = rank_ref[...] + start
    dest_ref[...] = dest
    dest_vmem[...] = dest
    to_smem = pltpu.make_async_copy(dest_vmem, dest_smem, sem_idx)
    to_smem.start()
    to_smem.wait()

    def row_copy(tok, k):
        return pltpu.make_async_copy(_row_tile(xn3_ref, tok), _row_tile(xs_out, dest_smem[k, tok]), sem_rows)

    def issue(tok, carry):
        for k in range(TOP_K):
            row_copy(tok, k).start(priority=k % 2)
        return carry

    lax.fori_loop(0, tm, issue, 0, unroll=True)

    def drain_rows(tok, carry):
        for k in range(TOP_K):
            row_copy(tok, k).wait()
        return carry

    lax.fori_loop(0, tm, drain_rows, 0, unroll=8)


def _dispatch(pstart, cnt, eidx, ranks, xn3, n_rows, tm=256):
    t = eidx.shape[1]
    meta = pl.BlockSpec((SUBLANES, tm), lambda i, ps, cn: (0, i))
    any_spec = pl.BlockSpec(memory_space=pl.ANY)
    grid_spec = pltpu.PrefetchScalarGridSpec(
        num_scalar_prefetch=2,
        grid=(t // tm,),
        in_specs=[meta, meta, pl.BlockSpec((tm * SUBLANES, LANES), lambda i, ps, cn: (i, 0))],
        out_specs=[meta, any_spec],
        scratch_shapes=[pltpu.VMEM((SUBLANES, tm), I32), pltpu.SMEM((SUBLANES, tm), I32),
                        pltpu.VMEM((ROW_BLOCK * SUBLANES, LANES), F32),
                        pltpu.SemaphoreType.DMA, pltpu.SemaphoreType.DMA, pltpu.SemaphoreType.DMA,
                        pltpu.SemaphoreType.DMA],
    )
    return pl.pallas_call(
        _dispatch_kernel,
        grid_spec=grid_spec,
        out_shape=[jax.ShapeDtypeStruct((SUBLANES, t), I32),
                   jax.ShapeDtypeStruct((n_rows * SUBLANES, LANES), F32)],
        compiler_params=_cparams(("arbitrary",)),
        name="dispatch",
    )(pstart, cnt, eidx, ranks, xn3)


def _experts_kernel(bexp_ref, nblk_ref, nbe_ref, xs_ref, bup_ref, bdn_ref, wup_hbm, wdn_hbm, y_ref,
                    wup32, wdn32, wup16, wdn16, x16, sem_up, sem_dn):
    i = pl.program_id(0)
    rows = x16.shape[0]
    d = x16.shape[1]
    de = wdn16.shape[0]
    e = bexp_ref[i]
    active = i < nblk_ref[0]
    first = (i == 0) | (e != bexp_ref[jnp.maximum(i - 1, 0)])

    def fetch(ex):
        return (pltpu.make_async_copy(wup_hbm.at[ex], wup32, sem_up),
                pltpu.make_async_copy(wdn_hbm.at[ex], wdn32, sem_dn))

    @pl.when(i == 0)
    def _():
        for cp in fetch(e):
            cp.start()

    @pl.when(active & first)
    def _():
        for cp in fetch(e):
            cp.wait()
        wup16[...] = wup32[...].astype(BF16)
        wdn16[...] = wdn32[...].astype(BF16)
        nxt = i + nbe_ref[e]

        @pl.when(nxt < nblk_ref[0])
        def _():
            for cp in fetch(bexp_ref[nxt]):
                cp.start()

    @pl.when(active)
    def _():
        for j in range(d // LANES):
            x16[:, j * LANES:(j + 1) * LANES] = xs_ref[pl.ds(j, rows, stride=SUBLANES), :].astype(BF16)
        hdn = jnp.dot(x16[...], wup16[...], preferred_element_type=F32) + bup_ref[0]
        g = jnp.minimum(hdn[:, :de], SWIGLU_LIMIT)
        u = jnp.clip(hdn[:, de:], -SWIGLU_LIMIT, SWIGLU_LIMIT)
        glu = g * jax.nn.sigmoid(SWIGLU_ALPHA * g)
        act = ((u + 1.0) * glu).astype(BF16)
        y = jnp.dot(act, wdn16[...], preferred_element_type=F32) + bdn_ref[0]
        for j in range(d // LANES):
            y_ref[pl.ds(j, rows, stride=SUBLANES), :] = y[:, j * LANES:(j + 1) * LANES]

    @pl.when(jnp.logical_not(active))
    def _():
        y_ref[...] = jnp.zeros_like(y_ref)


def _experts(bexp, nblk, nbe, xs, w_up, b_up, w_down, b_down):
    n_exp, d, d2 = w_up.shape
    de = w_down.shape[1]
    n_blocks = bexp.shape[0]
    rows = ROW_BLOCK
    xspec = pl.BlockSpec((rows * SUBLANES, LANES), lambda i, be, nb, ne: (i, 0))
    any_spec = pl.BlockSpec(memory_space=pl.ANY)
    grid_spec = pltpu.PrefetchScalarGridSpec(
        num_scalar_prefetch=3,
        grid=(n_blocks,),
        in_specs=[xspec,
                  pl.BlockSpec((1, 1, d2), lambda i, be, nb, ne: (be[i], 0, 0)),
                  pl.BlockSpec((1, 1, d), lambda i, be, nb, ne: (be[i], 0, 0)),
                  any_spec, any_spec],
        out_specs=xspec,
        scratch_shapes=[pltpu.VMEM((d, d2), F32), pltpu.VMEM((de, d), F32),
                        pltpu.VMEM((d, d2), BF16), pltpu.VMEM((de, d), BF16), pltpu.VMEM((rows, d), BF16),
                        pltpu.SemaphoreType.DMA, pltpu.SemaphoreType.DMA],
    )
    return pl.pallas_call(
        _experts_kernel,
        grid_spec=grid_spec,
        out_shape=jax.ShapeDtypeStruct(xs.shape, F32),
        compiler_params=_cparams(("arbitrary",)),
        name="experts",
    )(bexp, nblk, nbe, xs, b_up[:, None, :], b_down[:, None, :], w_up, w_down)


def _combine_kernel(dest_ref, dest_next_ref, gate_ref, h_ref, p_ref, pg_ref, wpg_ref, wp_ref, fg_ref, y_hbm,
                    out_ref, dest_smem, ybuf, sem_idx, sem_rows):
    tm = h_ref.shape[0]
    d = h_ref.shape[1]
    i = pl.program_id(0)
    n = pl.num_programs(0)
    slot = lax.rem(i, 2)

    def row_copy(tok, k, sl):
        return pltpu.make_async_copy(_row_tile(y_hbm, dest_smem[k, tok]), _row_tile(ybuf.at[sl, k], tok),
                                     sem_rows.at[sl])

    def gather(idx_ref, sl):
        to_smem = pltpu.make_async_copy(idx_ref, dest_smem, sem_idx)
        to_smem.start()
        to_smem.wait()

        def issue(tok, carry):
            for k in range(TOP_K):
                row_copy(tok, k, sl).start(priority=k % 2)
            return carry

        lax.fori_loop(0, tm, issue, 0, unroll=True)

    @pl.when(i == 0)
    def _():
        gather(dest_ref, 0)

    @pl.when(i + 1 < n)
    def _():
        gather(dest_next_ref, 1 - slot)

    def drain(tok, carry):
        for k in range(TOP_K):
            pltpu.make_async_copy(_row_tile(y_hbm, 0), _row_tile(ybuf.at[slot, k], tok), sem_rows.at[slot]).wait()
        return carry

    lax.fori_loop(0, tm, drain, 0, unroll=8)

    gates_t = jnp.transpose(gate_ref[...])
    gate_cols = [jnp.broadcast_to(gates_t[:, k:k + 1], (tm, LANES)) for k in range(TOP_K)]
    pieces = []
    for j in range(d // LANES):
        acc = jnp.zeros((tm, LANES), F32)
        for k in range(TOP_K):
            acc = acc + gate_cols[k] * ybuf[slot, k, pl.ds(j, tm, stride=SUBLANES), :]
        pieces.append(acc)
    h = h_ref[...] + jnp.concatenate(pieces, axis=1)

    ple = jnp.dot(p_ref[...].astype(BF16), wp_ref[...], preferred_element_type=F32)
    ple_gate = jax.nn.sigmoid(jnp.dot(_rms(h, pg_ref[...]).astype(BF16), wpg_ref[...],
                                      preferred_element_type=F32))
    h = h + ple * ple_gate
    out_ref[...] = _rms(h, fg_ref[...])


def _combine(dest, gates, h1, p2, pg, wpg, wp, fg, y, tm=256):
    t, d = h1.shape
    dp = p2.shape[1]
    row = lambda i: (i, 0)
    col = lambda i: (0, i)
    const = lambda i: (0, 0)
    return pl.pallas_call(
        _combine_kernel,
        grid=(t // tm,),
        in_specs=[pl.BlockSpec((SUBLANES, tm), col),
                  pl.BlockSpec((SUBLANES, tm), lambda i: (0, jnp.minimum(i + 1, t // tm - 1))),
                  pl.BlockSpec((SUBLANES, tm), col),
                  pl.BlockSpec((tm, d), row), pl.BlockSpec((tm, dp), row),
                  pl.BlockSpec((1, d), const), pl.BlockSpec((d, d), const), pl.BlockSpec((dp, d), const),
                  pl.BlockSpec((1, d), const), pl.BlockSpec(memory_space=pl.ANY)],
        out_specs=pl.BlockSpec((tm, d), row),
        out_shape=jax.ShapeDtypeStruct((t, d), F32),
        scratch_shapes=[pltpu.SMEM((SUBLANES, tm), I32), pltpu.VMEM((2, TOP_K, tm * SUBLANES, LANES), F32),
                        pltpu.SemaphoreType.DMA, pltpu.SemaphoreType.DMA((2,))],
        compiler_params=_cparams(("arbitrary",)),
        name="combine",
    )(dest, dest, gates, h1, p2, pg, wpg, wp, fg, y)


def _block_diag(w):
    n, d, _ = w.shape
    eye = jnp.eye(n, dtype=w.dtype)
    return (eye[:, None, :, None] * w[:, :, None, :]).reshape(n * d, n * d)


def _stages(x, p, mix_norm_g, w_in, conv_w, conv_b, lru_w_a, lru_b_a, lru_w_x, lru_b_x, lru_lambda,
            lru_out_g, sb_out_g, w_out, ffn_norm_g, w_router, b_router, w_up, b_up, w_down, b_down,
            ple_norm_g, w_ple_gate, w_ple, final_norm_g):
    b, s, d = x.shape
    t = b * s
    st = {}
    x2 = x.reshape(t, d)
    lx, lg, q, k, v = _in_proj(x2, mix_norm_g[0][None], w_in[0].astype(BF16))
    st.update(lru_x=lx, lru_gate=lg, q=q, k=k, v=v)
    row = lambda a: a[None].astype(F32)
    lru_n = _lru(lx.reshape(b, s, D_LRU), lg.reshape(b, s, D_LRU), conv_w[0], row(conv_b[0]),
                 _block_diag(lru_w_a[0]).astype(BF16), row(lru_b_a[0]),
                 _block_diag(lru_w_x[0]).astype(BF16), row(lru_b_x[0]),
                 row(lru_lambda[0]), row(lru_out_g[0]))
    st["lru_n"] = lru_n
    sb_y = _sb_attn(q.reshape(b, s, D_SB), k.reshape(b, s, D_SB), v.reshape(b, s, D_SB))
    st["sb_y"] = sb_y
    wo = w_out[0].astype(BF16)
    h1, xn3, eidx, gates, ranks, counts = _out_route(
        x2, lru_n.reshape(t, D_LRU), sb_y.reshape(t, D_SB), row(sb_out_g[0]), wo[:D_LRU], wo[D_LRU:],
        row(ffn_norm_g[0]), w_router[0].T, b_router[0][:, None])
    st.update(h1=h1, xn1=xn3, eidx=eidx, gates=gates, ranks=ranks, counts=counts)

    cnt = counts[:, 0].astype(I32)
    padded = (cnt + ROW_BLOCK - 1) // ROW_BLOCK * ROW_BLOCK
    pend = jnp.cumsum(padded)
    pstart = pend - padded
    n_blocks = -(-(t * TOP_K) // ROW_BLOCK) + N_EXPERTS
    block_row0 = jnp.arange(n_blocks, dtype=I32) * ROW_BLOCK
    bexp = jnp.minimum(jnp.sum((pend[None, :] <= block_row0[:, None]).astype(I32), axis=1), N_EXPERTS - 1)
    nblk = (pend[-1:] // ROW_BLOCK).astype(I32)

    dest, xs = _dispatch(pstart, cnt, eidx, ranks, xn3, n_blocks * ROW_BLOCK)
    y = _experts(bexp, nblk, padded // ROW_BLOCK, xs, w_up[0], b_up[0], w_down[0], b_down[0])
    st.update(dest=dest, xs=xs, y=y)
    out = _combine(dest, gates, h1, p[0].reshape(t, -1), row(ple_norm_g[0]), w_ple_gate[0].astype(BF16),
                   w_ple[0].astype(BF16), row(final_norm_g), y)
    st["final"] = out.reshape(b, s, d)
    return st


def kernel(x, p, mix_norm_g, w_in, conv_w, conv_b, lru_w_a, lru_b_a, lru_w_x, lru_b_x, lru_lambda,
           lru_out_g, sb_out_g, w_out, ffn_norm_g, w_router, b_router, w_up, b_up, w_down, b_down,
           ple_norm_g, w_ple_gate, w_ple, final_norm_g):
    return _stages(x, p, mix_norm_g, w_in, conv_w, conv_b, lru_w_a, lru_b_a, lru_w_x, lru_b_x, lru_lambda,
                   lru_out_g, sb_out_g, w_out, ffn_norm_g, w_router, b_router, w_up, b_up, w_down, b_down,
                   ple_norm_g, w_ple_gate, w_ple, final_norm_g)["final"]
```

```python
import functools

import jax
import jax.numpy as jnp
from jax import lax
from jax.experimental import pallas as pl
from jax.experimental.pallas import tpu as pltpu

F32 = jnp.float32
BF16 = jnp.bfloat16
I32 = jnp.int32

RMS_EPS = 1e-6
LANES = 128
SUBLANES = 8
D_LRU = 512
D_SB = 512
N_HEADS = 8
HEAD_DIM = 64
CONV_WIDTH = 4
LRU_C = 8.0
N_EXPERTS = 32
TOP_K = 4
SWIGLU_LIMIT = 7.0
SWIGLU_ALPHA = 1.702
ROW_BLOCK = 256
ROUTE_TILE = 512
RUN_SHIFT = 4
RUN_CHUNK = 1 << RUN_SHIFT
VMEM_LIMIT = 56 * 1024 * 1024


def _rms(x, g):
    return (x * lax.rsqrt(jnp.mean(x * x, axis=-1, keepdims=True) + RMS_EPS)) * g


def _cparams(sem, flags=None):
    return pltpu.CompilerParams(dimension_semantics=sem, vmem_limit_bytes=VMEM_LIMIT, flags=flags)


def _in_proj_kernel(x_ref, g_ref, w_ref, lx_ref, lg_ref, q_ref, k_ref, v_ref):
    xn = _rms(x_ref[...], g_ref[...])
    proj = jnp.dot(xn.astype(BF16), w_ref[...], preferred_element_type=F32)
    lx_ref[...] = proj[:, 0:D_LRU]
    lg_ref[...] = proj[:, D_LRU:2 * D_LRU]
    o = 2 * D_LRU
    q_ref[...] = proj[:, o:o + D_SB].astype(BF16)
    k_ref[...] = proj[:, o + D_SB:o + 2 * D_SB].astype(BF16)
    v_ref[...] = proj[:, o + 2 * D_SB:o + 3 * D_SB].astype(BF16)


def _in_proj(x2, g, w_bf, tm=512):
    t, d = x2.shape
    n = w_bf.shape[1]
    row = lambda i: (i, 0)
    const = lambda i: (0, 0)
    return pl.pallas_call(
        _in_proj_kernel,
        grid=(t // tm,),
        in_specs=[pl.BlockSpec((tm, d), row), pl.BlockSpec((1, d), const), pl.BlockSpec((d, n), const)],
        out_specs=[pl.BlockSpec((tm, D_LRU), row)] * 2 + [pl.BlockSpec((tm, D_SB), row)] * 3,
        out_shape=[jax.ShapeDtypeStruct((t, D_LRU), F32)] * 2 + [jax.ShapeDtypeStruct((t, D_SB), BF16)] * 3,
        compiler_params=_cparams(("parallel",)),
        name="in_proj",
    )(x2, g, w_bf)


def _shift_rows(x, k, fill):
    if k % SUBLANES == 0:
        return jnp.concatenate([jnp.full((k, x.shape[1]), fill, x.dtype), x[:x.shape[0] - k]], axis=0)
    rolled = pltpu.roll(x, k, 0)
    rows = lax.broadcasted_iota(I32, x.shape, 0)
    return jnp.where(rows >= k, rolled, fill)


def _lru_kernel(lx_ref, lg_ref, cw_ref, cb_ref, wa_ref, ba_ref, wx_ref, bx_ref, lam_ref, og_ref,
                out_ref, tail_ref, h_ref):
    ts = lx_ref.shape[1]

    @pl.when(pl.program_id(1) == 0)
    def _():
        tail_ref[...] = jnp.zeros_like(tail_ref)
        h_ref[...] = jnp.zeros_like(h_ref)

    x = lx_ref[0]
    tail = tail_ref[...]
    rows = lax.broadcasted_iota(I32, x.shape, 0)
    cw = cw_ref[...]
    conv = x * cw[CONV_WIDTH - 1:CONV_WIDTH, :] + cb_ref[...]
    for k in range(1, CONV_WIDTH):
        cur = pltpu.roll(x, k, 0)
        prev = pltpu.roll(tail, k, 0)
        prev_full = jnp.concatenate([prev] + [prev] * (ts // SUBLANES - 1), axis=0)
        shifted = jnp.where(rows >= k, cur, prev_full)
        conv = conv + shifted * cw[CONV_WIDTH - 1 - k:CONV_WIDTH - k, :]
    tail_ref[...] = x[ts - SUBLANES:, :]

    cb16 = conv.astype(BF16)
    r = jax.nn.sigmoid(jnp.dot(cb16, wa_ref[...], preferred_element_type=F32) + ba_ref[...])
    gi = jax.nn.sigmoid(jnp.dot(cb16, wx_ref[...], preferred_element_type=F32) + bx_ref[...])
    lam = lam_ref[...]
    softplus_neg = jnp.maximum(-lam, 0.0) + jnp.log1p(jnp.exp(-jnp.abs(lam)))
    log_a = (-LRU_C * r) * softplus_neg
    a = jnp.exp(log_a)
    b = jnp.sqrt(1.0 - jnp.exp(2.0 * log_a)) * (gi * conv)

    k = 1
    while k < ts:
        a_sh = _shift_rows(a, k, 1.0)
        b_sh = _shift_rows(b, k, 0.0)
        b = a * b_sh + b
        a = a * a_sh
        k *= 2
    h = a * h_ref[0:1, :] + b
    h_ref[...] = jnp.broadcast_to(h[ts - 1:ts, :], h_ref.shape)

    gate = lg_ref[0]
    y = h * jax.nn.gelu(gate)
    out_ref[0] = _rms(y, og_ref[...]).astype(out_ref.dtype)


def _lru(lx, lg, cw, cb, wa, ba, wx, bx, lam, og, ts=256):
    b, s, c = lx.shape
    tile = lambda i, j: (i, j, 0)
    const = lambda i, j: (0, 0)
    vec = pl.BlockSpec((1, c), const)
    return pl.pallas_call(
        _lru_kernel,
        grid=(b, s // ts),
        in_specs=[pl.BlockSpec((1, ts, c), tile), pl.BlockSpec((1, ts, c), tile),
                  pl.BlockSpec((CONV_WIDTH, c), const), vec,
                  pl.BlockSpec((c, c), const), vec, pl.BlockSpec((c, c), const), vec, vec, vec],
        out_specs=pl.BlockSpec((1, ts, c), tile),
        out_shape=jax.ShapeDtypeStruct((b, s, c), BF16),
        scratch_shapes=[pltpu.VMEM((SUBLANES, c), F32), pltpu.VMEM((SUBLANES, c), F32)],
        compiler_params=_cparams(("parallel", "arbitrary")),
        name="lru",
    )(lx, lg, cw, cb, wa, ba, wx, bx, lam, og)


def _split_bf16(x):
    hi = x.astype(BF16)
    lo = (x - hi.astype(F32)).astype(BF16)
    return hi, lo


def _sb_attn_kernel(q_ref, k_ref, v_ref, o_ref, acc_ref, run_ref, *, tq, tk, slabs):
    qi = pl.program_id(2)
    lane = lax.broadcasted_iota(I32, (1, LANES), 1)
    head_masks = [lane < HEAD_DIM, lane >= HEAD_DIM]
    zero = jnp.zeros((), BF16)
    qs = []
    for sl in range(slabs):
        q = q_ref[0, :, sl * LANES:(sl + 1) * LANES] * jnp.asarray(HEAD_DIM ** -0.5, BF16)
        qs.append([jnp.where(m, q, zero) for m in head_masks])

    r = lax.broadcasted_iota(I32, (tk, tk), 0)
    c = lax.broadcasted_iota(I32, (tk, tk), 1)
    suffix = jnp.where(r > c, 1.0, 0.0).astype(BF16)

    acc_ref[...] = jnp.zeros_like(acc_ref)
    run_ref[...] = jnp.zeros_like(run_ref)

    def tile(j, diagonal):
        rows = pl.ds(pl.multiple_of(j * tk, tk), tk)
        if diagonal:
            qpos = lax.broadcasted_iota(I32, (tq, tk), 0)
            kpos = lax.broadcasted_iota(I32, (tq, tk), 1)
            visible = kpos < qpos
        for sl in range(slabs):
            ks = k_ref[0, rows, sl * LANES:(sl + 1) * LANES]
            vs = v_ref[0, rows, sl * LANES:(sl + 1) * LANES]
            pv = None
            for hd in range(2):
                z = lax.dot_general(qs[sl][hd], ks, (((1,), (1,)), ((), ())), preferred_element_type=F32)
                drop = jnp.maximum(z, 0.0) + jnp.log(1.0 + jnp.exp(-jnp.abs(z)))
                if diagonal:
                    drop = jnp.where(visible, drop, 0.0)
                sums = jnp.dot(drop.astype(BF16), suffix, preferred_element_type=F32)
                run = run_ref[2 * sl + hd]
                w = jnp.exp(z - ((drop + sums) + run))
                if diagonal:
                    w = jnp.where(visible, w, 0.0)
                vh = jnp.where(head_masks[hd], vs, zero)
                part = jnp.dot(w.astype(BF16), vh, preferred_element_type=F32)
                pv = part if pv is None else pv + part
                run_ref[2 * sl + hd] = run + (sums[:, 0:1] + drop[:, 0:1])
            acc_ref[:, sl * LANES:(sl + 1) * LANES] += pv

    tile(qi, True)

    def body(i, carry):
        tile(qi - 1 - i, False)
        return carry

    lax.fori_loop(0, qi, body, 0)
    o_ref[0] = acc_ref[...]


def _sb_attn(q, k, v, tq=512, slabs=2):
    b, s, c = q.shape
    tk = tq
    w = LANES * slabs
    qspec = pl.BlockSpec((1, tq, w), lambda i, j, l: (i, l, j))
    kvspec = pl.BlockSpec((1, s, w), lambda i, j, l: (i, 0, j))
    return pl.pallas_call(
        functools.partial(_sb_attn_kernel, tq=tq, tk=tk, slabs=slabs),
        grid=(b, c // w, s // tq),
        in_specs=[qspec, kvspec, kvspec],
        out_specs=qspec,
        out_shape=jax.ShapeDtypeStruct((b, s, c), F32),
        scratch_shapes=[pltpu.VMEM((tq, w), F32), pltpu.VMEM((2 * slabs, tq, 1), F32)],
        compiler_params=_cparams(("parallel", "parallel", "arbitrary")),
        name="sb_attn",
    )(q, k, v)


def _out_route_kernel(x_ref, lru_ref, sb_ref, sbg_ref, wol_ref, wos_ref, fg_ref, wr_ref, br_ref,
                      h_ref, xn3_ref, eidx_ref, gate_ref, rank_ref, cnt_ref, tbase_ref, cnt_scr):
    tm = x_ref.shape[0]

    @pl.when(pl.program_id(0) == 0)
    def _():
        cnt_scr[...] = jnp.zeros_like(cnt_scr)

    sbn = _rms(sb_ref[...], sbg_ref[...]).astype(BF16)
    h = (x_ref[...] + jnp.dot(lru_ref[...], wol_ref[...], preferred_element_type=F32)
         + jnp.dot(sbn, wos_ref[...], preferred_element_type=F32))
    h_ref[...] = h
    xn = _rms(h, fg_ref[...])
    for j in range(xn.shape[1] // LANES):
        xn3_ref[pl.ds(j, tm, stride=SUBLANES), :] = xn[:, j * LANES:(j + 1) * LANES]

    nt = (((1,), (1,)), ((), ()))
    w_hi, w_lo = _split_bf16(wr_ref[...])
    x_hi, x_lo = _split_bf16(xn)
    logits = (lax.dot_general(w_hi, x_hi, nt, preferred_element_type=F32)
              + (lax.dot_general(w_hi, x_lo, nt, preferred_element_type=F32)
                 + lax.dot_general(w_lo, x_hi, nt, preferred_element_type=F32))) + br_ref[...]
    n_exp = logits.shape[0]
    eio = lax.broadcasted_iota(I32, logits.shape, 0)
    work = logits
    vals, hits, idxs = [], [], []
    for _k in range(TOP_K):
        m = jnp.max(work, axis=0, keepdims=True)
        idx = jnp.min(jnp.where(work == m, eio, n_exp), axis=0, keepdims=True)
        hit = eio == idx
        work = jnp.where(hit, -jnp.inf, work)
        vals.append(m)
        hits.append(hit)
        idxs.append(idx)
    exps = [jnp.exp(v - vals[0]) for v in vals]
    denom = exps[0] + exps[1] + exps[2] + exps[3]
    gates = [e / denom for e in exps]

    onehot = jnp.where(hits[0] | hits[1] | hits[2] | hits[3], 1.0, 0.0).astype(BF16)
    r = lax.broadcasted_iota(I32, (tm, tm + LANES), 0)
    c = lax.broadcasted_iota(I32, (tm, tm + LANES), 1)
    prefix_mat = jnp.where((r < c) | (c >= tm), 1.0, 0.0).astype(BF16)
    sums = jnp.dot(onehot, prefix_mat, preferred_element_type=F32)
    base = cnt_scr[...]
    pos = sums[:, :tm] + jnp.concatenate([base] * (tm // LANES), axis=1)
    ranks = [jnp.sum(jnp.where(hk, pos, 0.0), axis=0, keepdims=True) for hk in hits]
    tbase_ref[...] = base
    cnt_scr[...] = base + sums[:, tm:]
    cnt_ref[...] = cnt_scr[...]

    pad_i = jnp.zeros((SUBLANES - TOP_K, tm), I32)
    eidx_ref[...] = jnp.concatenate(idxs + [pad_i], axis=0)
    rank_ref[...] = jnp.concatenate([rk.astype(I32) for rk in ranks] + [pad_i], axis=0)
    gate_ref[...] = jnp.concatenate(gates + [jnp.zeros((SUBLANES - TOP_K, tm), F32)], axis=0)


def _out_route(x2, lru_n, sb_y, sbg, wol, wos, fg, wr_t, br, tm=ROUTE_TILE):
    t, d = x2.shape
    e = wr_t.shape[0]
    row = lambda i: (i, 0)
    col = lambda i: (0, i)
    const = lambda i: (0, 0)
    meta = pl.BlockSpec((SUBLANES, tm), col)
    return pl.pallas_call(
        _out_route_kernel,
        grid=(t // tm,),
        in_specs=[pl.BlockSpec((tm, d), row), pl.BlockSpec((tm, D_LRU), row), pl.BlockSpec((tm, D_SB), row),
                  pl.BlockSpec((1, D_SB), const), pl.BlockSpec((D_LRU, d), const), pl.BlockSpec((D_SB, d), const),
                  pl.BlockSpec((1, d), const), pl.BlockSpec((e, d), const), pl.BlockSpec((e, 1), const)],
        out_specs=[pl.BlockSpec((tm, d), row), pl.BlockSpec((tm * SUBLANES, LANES), row), meta, meta, meta,
                   pl.BlockSpec((e, LANES), const), pl.BlockSpec((e, LANES), row)],
        out_shape=[jax.ShapeDtypeStruct((t, d), F32), jax.ShapeDtypeStruct((t * SUBLANES, LANES), F32),
                   jax.ShapeDtypeStruct((SUBLANES, t), I32), jax.ShapeDtypeStruct((SUBLANES, t), F32),
                   jax.ShapeDtypeStruct((SUBLANES, t), I32), jax.ShapeDtypeStruct((e, LANES), F32),
                   jax.ShapeDtypeStruct((t // tm * e, LANES), F32)],
        scratch_shapes=[pltpu.VMEM((e, LANES), F32)],
        compiler_params=_cparams(("arbitrary",)),
        name="out_route",
    )(x2, lru_n, sb_y, sbg, wol, wos, fg, wr_t, br)


def _row_tile(ref, row):
    return ref.at[pl.ds(pl.multiple_of(row * SUBLANES, SUBLANES), SUBLANES)]


def _dispatch_kernel(pstart_ref, cnt_ref, eidx_ref, rank_ref, xn3_ref, dest_ref, xs_out,
                     dest_vmem, dest_smem, zero_tile, sem_idx, sem_rows, sem_pad, sem_blk):
    tm = eidx_ref.shape[1]

    @pl.when(pl.program_id(0) == 0)
    def _():
        zero_tile[...] = jnp.zeros_like(zero_tile)

        def pad_copy(row):
            return pltpu.make_async_copy(zero_tile.at[pl.ds(0, SUBLANES)], _row_tile(xs_out, row), sem_pad)

        def pads_of(ex, action):
            first_pad = pstart_ref[ex] + cnt_ref[ex]
            n_pad = (0 - cnt_ref[ex]) & (ROW_BLOCK - 1)

            def one(r, c):
                action(pad_copy(first_pad + r))
                return c

            lax.fori_loop(0, n_pad, one, 0)

        def fill_expert(ex, carry):
            pads_of(ex, lambda cp: cp.start())
            return carry

        def drain_expert(ex, carry):
            pads_of(ex, lambda cp: cp.wait())
            return carry

        lax.fori_loop(0, N_EXPERTS, fill_expert, 0)

        last = N_EXPERTS - 1
        used_rows = pstart_ref[last] + cnt_ref[last] + ((0 - cnt_ref[last]) & (ROW_BLOCK - 1))
        block_rows = ROW_BLOCK * SUBLANES

        def block_copy(blk):
            return pltpu.make_async_copy(
                zero_tile, xs_out.at[pl.ds(pl.multiple_of(blk * block_rows, block_rows), block_rows)], sem_blk)

        def fill_block(blk, c):
            block_copy(blk).start()
            return c

        def drain_block(blk, c):
            block_copy(blk).wait()
            return c

        first_free = used_rows // ROW_BLOCK
        lax.fori_loop(first_free, xs_out.shape[0] // block_rows, fill_block, 0)
        lax.fori_loop(0, N_EXPERTS, drain_expert, 0)
        lax.fori_loop(first_free, xs_out.shape[0] // block_rows, drain_block, 0)

    e = eidx_ref[...]
    start = jnp.zeros_like(e)
    for ex in range(N_EXPERTS):
        start = jnp.where(e == ex, pstart_ref[ex], start)
    dest = rank_ref[...] + start
    dest_ref[...] = dest
    dest_vmem[...] = dest
    to_smem = pltpu.make_async_copy(dest_vmem, dest_smem, sem_idx)
    to_smem.start()
    to_smem.wait()

    def row_copy(tok, k):
        return pltpu.make_async_copy(_row_tile(xn3_ref, tok), _row_tile(xs_out, dest_smem[k, tok]), sem_rows)

    def issue(tok, carry):
        for k in range(TOP_K):
            row_copy(tok, k).start(priority=k % 2)
        return carry

    lax.fori_loop(0, tm, issue, 0, unroll=True)

    def drain_rows(tok, carry):
        for k in range(TOP_K):
            row_copy(tok, k).wait()
        return carry

    lax.fori_loop(0, tm, drain_rows, 0, unroll=8)


def _dispatch(pstart, cnt, eidx, ranks, xn3, n_rows, tm=256):
    t = eidx.shape[1]
    meta = pl.BlockSpec((SUBLANES, tm), lambda i, ps, cn: (0, i))
    any_spec = pl.BlockSpec(memory_space=pl.ANY)
    grid_spec = pltpu.PrefetchScalarGridSpec(
        num_scalar_prefetch=2,
        grid=(t // tm,),
        in_specs=[meta, meta, pl.BlockSpec((tm * SUBLANES, LANES), lambda i, ps, cn: (i, 0))],
        out_specs=[meta, any_spec],
        scratch_shapes=[pltpu.VMEM((SUBLANES, tm), I32), pltpu.SMEM((SUBLANES, tm), I32),
                        pltpu.VMEM((ROW_BLOCK * SUBLANES, LANES), F32),
                        pltpu.SemaphoreType.DMA, pltpu.SemaphoreType.DMA, pltpu.SemaphoreType.DMA,
                        pltpu.SemaphoreType.DMA],
    )
    return pl.pallas_call(
        _dispatch_kernel,
        grid_spec=grid_spec,
        out_shape=[jax.ShapeDtypeStruct((SUBLANES, t), I32),
                   jax.ShapeDtypeStruct((n_rows * SUBLANES, LANES), F32)],
        compiler_params=_cparams(("arbitrary",)),
        name="dispatch",
    )(pstart, cnt, eidx, ranks, xn3)


def _experts_kernel(bexp_ref, nblk_ref, nbe_ref, xs_ref, bup_ref, bdn_ref, wup_hbm, wdn_hbm, y_ref,
                    wup32, wdn32, wup16, wdn16, x16, sem_up, sem_dn):
    i = pl.program_id(0)
    rows = x16.shape[0]
    d = x16.shape[1]
    de = wdn16.shape[0]
    e = bexp_ref[i]
    active = i < nblk_ref[0]
    first = (i == 0) | (e != bexp_ref[jnp.maximum(i - 1, 0)])

    def fetch(ex):
        return (pltpu.make_async_copy(wup_hbm.at[ex], wup32, sem_up),
                pltpu.make_async_copy(wdn_hbm.at[ex], wdn32, sem_dn))

    @pl.when(i == 0)
    def _():
        for cp in fetch(e):
            cp.start()

    @pl.when(active & first)
    def _():
        for cp in fetch(e):
            cp.wait()
        wup16[...] = wup32[...].astype(BF16)
        wdn16[...] = wdn32[...].astype(BF16)
        nxt = i + nbe_ref[e]

        @pl.when(nxt < nblk_ref[0])
        def _():
            for cp in fetch(bexp_ref[nxt]):
                cp.start()

    @pl.when(active)
    def _():
        for j in range(d // LANES):
            x16[:, j * LANES:(j + 1) * LANES] = xs_ref[pl.ds(j, rows, stride=SUBLANES), :].astype(BF16)
        hdn = jnp.dot(x16[...], wup16[...], preferred_element_type=F32) + bup_ref[0]
        g = jnp.minimum(hdn[:, :de], SWIGLU_LIMIT)
        u = jnp.clip(hdn[:, de:], -SWIGLU_LIMIT, SWIGLU_LIMIT)
        glu = g * jax.nn.sigmoid(SWIGLU_ALPHA * g)
        act = ((u + 1.0) * glu).astype(BF16)
        y = jnp.dot(act, wdn16[...], preferred_element_type=F32) + bdn_ref[0]
        for j in range(d // LANES):
            y_ref[pl.ds(j, rows, stride=SUBLANES), :] = y[:, j * LANES:(j + 1) * LANES]

    @pl.when(jnp.logical_not(active))
    def _():
        y_ref[...] = jnp.zeros_like(y_ref)


def _experts(bexp, nblk, nbe, xs, w_up, b_up, w_down, b_down):
    n_exp, d, d2 = w_up.shape
    de = w_down.shape[1]
    n_blocks = bexp.shape[0]
    rows = ROW_BLOCK
    xspec = pl.BlockSpec((rows * SUBLANES, LANES), lambda i, be, nb, ne: (i, 0))
    any_spec = pl.BlockSpec(memory_space=pl.ANY)
    grid_spec = pltpu.PrefetchScalarGridSpec(
        num_scalar_prefetch=3,
        grid=(n_blocks,),
        in_specs=[xspec,
                  pl.BlockSpec((1, 1, d2), lambda i, be, nb, ne: (be[i], 0, 0)),
                  pl.BlockSpec((1, 1, d), lambda i, be, nb, ne: (be[i], 0, 0)),
                  any_spec, any_spec],
        out_specs=xspec,
        scratch_shapes=[pltpu.VMEM((d, d2), F32), pltpu.VMEM((de, d), F32),
                        pltpu.VMEM((d, d2), BF16), pltpu.VMEM((de, d), BF16), pltpu.VMEM((rows, d), BF16),
                        pltpu.SemaphoreType.DMA, pltpu.SemaphoreType.DMA],
    )
    return pl.pallas_call(
        _experts_kernel,
        grid_spec=grid_spec,
        out_shape=jax.ShapeDtypeStruct(xs.shape, F32),
        compiler_params=_cparams(("arbitrary",)),
        name="experts",
    )(bexp, nblk, nbe, xs, b_up[:, None, :], b_down[:, None, :], w_up, w_down)


def _combine_kernel(rstart_ref, rcnt_ref, tbase_ref, eidx_ref, rank_ref, gate_ref, h_ref, p_ref, pg_ref, wpg_ref,
                    wp_ref, fg_ref, y_hbm, out_ref,
                    stage, li_vmem, li_smem, gate_smem, delta_smem, nchunk_smem, moe_rows, sem_idx, sem_runs):
    tm = h_ref.shape[0]
    d = h_ref.shape[1]
    i = pl.program_id(0)
    n = pl.num_programs(0)
    slot = lax.rem(i, 2)
    chunk_rows = RUN_CHUNK * SUBLANES
    slot_rows = stage.shape[0] // 2

    def chunks_of(c):
        return (c + (RUN_CHUNK - 1)) >> RUN_SHIFT

    def chunk_copy(src, dst, sl):
        return pltpu.make_async_copy(y_hbm.at[pl.ds(pl.multiple_of(src, SUBLANES), chunk_rows)],
                                     stage.at[pl.ds(pl.multiple_of(dst, SUBLANES), chunk_rows)], sem_runs.at[sl])

    def start_runs(tile, sl):
        def per_expert(ex, dst):
            src = rstart_ref[tile * N_EXPERTS + ex] * SUBLANES
            n_ch = chunks_of(rcnt_ref[tile * N_EXPERTS + ex])

            def one(ch, carry):
                chunk_copy(src + ch * chunk_rows, dst + ch * chunk_rows, sl).start()
                return carry

            lax.fori_loop(0, n_ch, one, 0)
            return dst + n_ch * chunk_rows

        end = lax.fori_loop(0, N_EXPERTS, per_expert, sl * slot_rows)
        nchunk_smem[sl] = (end - sl * slot_rows) >> (RUN_SHIFT + 3)

    @pl.when(i == 0)
    def _():
        start_runs(0, 0)

    @pl.when(i + 1 < n)
    def _():
        start_runs(i + 1, 1 - slot)

    def fill_delta(ex, off):
        delta_smem[ex] = off - tbase_ref[i * N_EXPERTS + ex]
        return off + chunks_of(rcnt_ref[i * N_EXPERTS + ex]) * RUN_CHUNK

    lax.fori_loop(0, N_EXPERTS, fill_delta, 0)
    e = eidx_ref[...]
    delta = jnp.zeros_like(e)
    for ex in range(N_EXPERTS):
        delta = jnp.where(e == ex, delta_smem[ex], delta)
    li_vmem[...] = (rank_ref[...] + delta) * SUBLANES + slot * slot_rows
    to_smem = [pltpu.make_async_copy(li_vmem, li_smem, sem_idx.at[0]),
               pltpu.make_async_copy(gate_ref, gate_smem, sem_idx.at[1])]
    for cp in to_smem:
        cp.start()
    for cp in to_smem:
        cp.wait()

    def wait_chunk(ch, carry):
        chunk_copy(0, slot * slot_rows, slot).wait()
        return carry

    lax.fori_loop(0, nchunk_smem[slot], wait_chunk, 0)

    for tok in range(tm):
        acc = None
        for k in range(TOP_K):
            row = stage[pl.ds(pl.multiple_of(li_smem[k, tok], SUBLANES), SUBLANES), :]
            term = gate_smem[k, tok] * row
            acc = term if acc is None else acc + term
        moe_rows[tok * SUBLANES:(tok + 1) * SUBLANES, :] = acc

    moe = jnp.concatenate([moe_rows[pl.ds(j, tm, stride=SUBLANES), :] for j in range(d // LANES)], axis=1)
    h = h_ref[...] + moe

    ple = jnp.dot(p_ref[...].astype(BF16), wp_ref[...], preferred_element_type=F32)
    ple_gate = jax.nn.sigmoid(jnp.dot(_rms(h, pg_ref[...]).astype(BF16), wpg_ref[...],
                                      preferred_element_type=F32))
    h = h + ple * ple_gate
    out_ref[...] = _rms(h, fg_ref[...])


def _combine(rstart, rcnt, tbase, eidx, ranks, gates, h1, p2, pg, wpg, wp, fg, y, tm=ROUTE_TILE):
    t, d = h1.shape
    dp = p2.shape[1]
    row = lambda i, *_: (i, 0)
    col = lambda i, *_: (0, i)
    const = lambda i, *_: (0, 0)
    meta = pl.BlockSpec((SUBLANES, tm), col)
    stage_rows = (TOP_K * tm + N_EXPERTS * RUN_CHUNK) * SUBLANES
    grid_spec = pltpu.PrefetchScalarGridSpec(
        num_scalar_prefetch=3,
        grid=(t // tm,),
        in_specs=[meta, meta, meta, pl.BlockSpec((tm, d), row), pl.BlockSpec((tm, dp), row),
                  pl.BlockSpec((1, d), const), pl.BlockSpec((d, d), const), pl.BlockSpec((dp, d), const),
                  pl.BlockSpec((1, d), const), pl.BlockSpec(memory_space=pl.ANY)],
        out_specs=pl.BlockSpec((tm, d), row),
        scratch_shapes=[pltpu.VMEM((2 * stage_rows, LANES), F32), pltpu.VMEM((SUBLANES, tm), I32),
                        pltpu.SMEM((SUBLANES, tm), I32), pltpu.SMEM((SUBLANES, tm), F32),
                        pltpu.SMEM((N_EXPERTS,), I32), pltpu.SMEM((2,), I32),
                        pltpu.VMEM((tm * SUBLANES, LANES), F32),
                        pltpu.SemaphoreType.DMA((2,)), pltpu.SemaphoreType.DMA((2,))],
    )
    return pl.pallas_call(
        _combine_kernel,
        grid_spec=grid_spec,
        out_shape=jax.ShapeDtypeStruct((t, d), F32),
        compiler_params=_cparams(("arbitrary",)),
        name="combine",
    )(rstart, rcnt, tbase, eidx, ranks, gates, h1, p2, pg, wpg, wp, fg, y)


def _block_diag(w):
    n, d, _ = w.shape
    eye = jnp.eye(n, dtype=w.dtype)
    return (eye[:, None, :, None] * w[:, :, None, :]).reshape(n * d, n * d)


def _stages(x, p, mix_norm_g, w_in, conv_w, conv_b, lru_w_a, lru_b_a, lru_w_x, lru_b_x, lru_lambda,
            lru_out_g, sb_out_g, w_out, ffn_norm_g, w_router, b_router, w_up, b_up, w_down, b_down,
            ple_norm_g, w_ple_gate, w_ple, final_norm_g):
    b, s, d = x.shape
    t = b * s
    st = {}
    x2 = x.reshape(t, d)
    lx, lg, q, k, v = _in_proj(x2, mix_norm_g[0][None], w_in[0].astype(BF16))
    st.update(lru_x=lx, lru_gate=lg, q=q, k=k, v=v)
    row = lambda a: a[None].astype(F32)
    lru_n = _lru(lx.reshape(b, s, D_LRU), lg.reshape(b, s, D_LRU), conv_w[0], row(conv_b[0]),
                 _block_diag(lru_w_a[0]).astype(BF16), row(lru_b_a[0]),
                 _block_diag(lru_w_x[0]).astype(BF16), row(lru_b_x[0]),
                 row(lru_lambda[0]), row(lru_out_g[0]))
    st["lru_n"] = lru_n
    sb_y = _sb_attn(q.reshape(b, s, D_SB), k.reshape(b, s, D_SB), v.reshape(b, s, D_SB))
    st["sb_y"] = sb_y
    wo = w_out[0].astype(BF16)
    h1, xn3, eidx, gates, ranks, counts, tile_base = _out_route(
        x2, lru_n.reshape(t, D_LRU), sb_y.reshape(t, D_SB), row(sb_out_g[0]), wo[:D_LRU], wo[D_LRU:],
        row(ffn_norm_g[0]), w_router[0].T, b_router[0][:, None])
    st.update(h1=h1, xn1=xn3, eidx=eidx, gates=gates, ranks=ranks, counts=counts)

    cnt = counts[:, 0].astype(I32)
    padded = (cnt + ROW_BLOCK - 1) // ROW_BLOCK * ROW_BLOCK
    pend = jnp.cumsum(padded)
    pstart = pend - padded
    n_blocks = -(-(t * TOP_K) // ROW_BLOCK) + N_EXPERTS
    block_row0 = jnp.arange(n_blocks, dtype=I32) * ROW_BLOCK
    bexp = jnp.minimum(jnp.sum((pend[None, :] <= block_row0[:, None]).astype(I32), axis=1), N_EXPERTS - 1)
    nblk = (pend[-1:] // ROW_BLOCK).astype(I32)

    dest, xs = _dispatch(pstart, cnt, eidx, ranks, xn3, n_blocks * ROW_BLOCK)
    y = _experts(bexp, nblk, padded // ROW_BLOCK, xs, w_up[0], b_up[0], w_down[0], b_down[0])
    st.update(dest=dest, xs=xs, y=y)

    tbase = tile_base[:, 0].astype(I32).reshape(-1, N_EXPERTS)
    rcnt = jnp.concatenate([tbase[1:], cnt[None, :]], axis=0) - tbase
    rstart = pstart[None, :] + tbase
    out = _combine(rstart.reshape(-1), rcnt.reshape(-1), tbase.reshape(-1), eidx, ranks, gates, h1,
                   p[0].reshape(t, -1), row(ple_norm_g[0]), w_ple_gate[0].astype(BF16),
                   w_ple[0].astype(BF16), row(final_norm_g), y)
    st["final"] = out.reshape(b, s, d)
    return st


def kernel(x, p, mix_norm_g, w_in, conv_w, conv_b, lru_w_a, lru_b_a, lru_w_x, lru_b_x, lru_lambda,
           lru_out_g, sb_out_g, w_out, ffn_norm_g, w_router, b_router, w_up, b_up, w_down, b_down,
           ple_norm_g, w_ple_gate, w_ple, final_norm_g):
    return _stages(x, p, mix_norm_g, w_in, conv_w, conv_b, lru_w_a, lru_b_a, lru_w_x, lru_b_x, lru_lambda,
                   lru_out_g, sb_out_g, w_out, ffn_norm_g, w_router, b_router, w_up, b_up, w_down, b_down,
                   ple_norm_g, w_ple_gate, w_ple, final_norm_g)["final"]
```

```python
import functools

import jax
import jax.numpy as jnp
from jax import lax
from jax.experimental import pallas as pl
from jax.experimental.pallas import tpu as pltpu

F32 = jnp.float32
BF16 = jnp.bfloat16
I32 = jnp.int32

RMS_EPS = 1e-6
LANES = 128
SUBLANES = 8
D_LRU = 512
D_SB = 512
N_HEADS = 8
HEAD_DIM = 64
CONV_WIDTH = 4
LRU_C = 8.0
N_EXPERTS = 32
TOP_K = 4
SWIGLU_LIMIT = 7.0
SWIGLU_ALPHA = 1.702
ROW_BLOCK = 256
ROUTE_TILE = 512
RUN_SHIFT = 4
RUN_CHUNK = 1 << RUN_SHIFT
VMEM_LIMIT = 56 * 1024 * 1024


def _rms(x, g):
    return (x * lax.rsqrt(jnp.mean(x * x, axis=-1, keepdims=True) + RMS_EPS)) * g


def _cparams(sem, flags=None):
    return pltpu.CompilerParams(dimension_semantics=sem, vmem_limit_bytes=VMEM_LIMIT, flags=flags)


def _in_proj_kernel(x_ref, g_ref, w_ref, lx_ref, lg_ref, q_ref, k_ref, v_ref):
    xn = _rms(x_ref[...], g_ref[...])
    proj = jnp.dot(xn.astype(BF16), w_ref[...], preferred_element_type=F32)
    lx_ref[...] = proj[:, 0:D_LRU]
    lg_ref[...] = proj[:, D_LRU:2 * D_LRU]
    o = 2 * D_LRU
    q_ref[...] = proj[:, o:o + D_SB].astype(BF16)
    k_ref[...] = proj[:, o + D_SB:o + 2 * D_SB].astype(BF16)
    v_ref[...] = proj[:, o + 2 * D_SB:o + 3 * D_SB].astype(BF16)


def _in_proj(x2, g, w_bf, tm=512):
    t, d = x2.shape
    n = w_bf.shape[1]
    row = lambda i: (i, 0)
    const = lambda i: (0, 0)
    return pl.pallas_call(
        _in_proj_kernel,
        grid=(t // tm,),
        in_specs=[pl.BlockSpec((tm, d), row), pl.BlockSpec((1, d), const), pl.BlockSpec((d, n), const)],
        out_specs=[pl.BlockSpec((tm, D_LRU), row)] * 2 + [pl.BlockSpec((tm, D_SB), row)] * 3,
        out_shape=[jax.ShapeDtypeStruct((t, D_LRU), F32)] * 2 + [jax.ShapeDtypeStruct((t, D_SB), BF16)] * 3,
        compiler_params=_cparams(("parallel",)),
        name="in_proj",
    )(x2, g, w_bf)


def _shift_rows(x, k, fill):
    if k % SUBLANES == 0:
        return jnp.concatenate([jnp.full((k, x.shape[1]), fill, x.dtype), x[:x.shape[0] - k]], axis=0)
    rolled = pltpu.roll(x, k, 0)
    rows = lax.broadcasted_iota(I32, x.shape, 0)
    return jnp.where(rows >= k, rolled, fill)


def _lru_kernel(lx_ref, lg_ref, cw_ref, cb_ref, wa_ref, ba_ref, wx_ref, bx_ref, lam_ref, og_ref,
                out_ref, tail_ref, h_ref):
    ts = lx_ref.shape[1]

    @pl.when(pl.program_id(1) == 0)
    def _():
        tail_ref[...] = jnp.zeros_like(tail_ref)
        h_ref[...] = jnp.zeros_like(h_ref)

    x = lx_ref[0]
    tail = tail_ref[...]
    rows = lax.broadcasted_iota(I32, x.shape, 0)
    cw = cw_ref[...]
    conv = x * cw[CONV_WIDTH - 1:CONV_WIDTH, :] + cb_ref[...]
    for k in range(1, CONV_WIDTH):
        cur = pltpu.roll(x, k, 0)
        prev = pltpu.roll(tail, k, 0)
        prev_full = jnp.concatenate([prev] + [prev] * (ts // SUBLANES - 1), axis=0)
        shifted = jnp.where(rows >= k, cur, prev_full)
        conv = conv + shifted * cw[CONV_WIDTH - 1 - k:CONV_WIDTH - k, :]
    tail_ref[...] = x[ts - SUBLANES:, :]

    cb16 = conv.astype(BF16)
    r = jax.nn.sigmoid(jnp.dot(cb16, wa_ref[...], preferred_element_type=F32) + ba_ref[...])
    gi = jax.nn.sigmoid(jnp.dot(cb16, wx_ref[...], preferred_element_type=F32) + bx_ref[...])
    lam = lam_ref[...]
    softplus_neg = jnp.maximum(-lam, 0.0) + jnp.log1p(jnp.exp(-jnp.abs(lam)))
    log_a = (-LRU_C * r) * softplus_neg
    a = jnp.exp(log_a)
    b = jnp.sqrt(1.0 - jnp.exp(2.0 * log_a)) * (gi * conv)

    k = 1
    while k < ts:
        a_sh = _shift_rows(a, k, 1.0)
        b_sh = _shift_rows(b, k, 0.0)
        b = a * b_sh + b
        a = a * a_sh
        k *= 2
    h = a * h_ref[0:1, :] + b
    h_ref[...] = jnp.broadcast_to(h[ts - 1:ts, :], h_ref.shape)

    gate = lg_ref[0]
    y = h * jax.nn.gelu(gate)
    out_ref[0] = _rms(y, og_ref[...]).astype(out_ref.dtype)


def _lru(lx, lg, cw, cb, wa, ba, wx, bx, lam, og, ts=256):
    b, s, c = lx.shape
    tile = lambda i, j: (i, j, 0)
    const = lambda i, j: (0, 0)
    vec = pl.BlockSpec((1, c), const)
    return pl.pallas_call(
        _lru_kernel,
        grid=(b, s // ts),
        in_specs=[pl.BlockSpec((1, ts, c), tile), pl.BlockSpec((1, ts, c), tile),
                  pl.BlockSpec((CONV_WIDTH, c), const), vec,
                  pl.BlockSpec((c, c), const), vec, pl.BlockSpec((c, c), const), vec, vec, vec],
        out_specs=pl.BlockSpec((1, ts, c), tile),
        out_shape=jax.ShapeDtypeStruct((b, s, c), BF16),
        scratch_shapes=[pltpu.VMEM((SUBLANES, c), F32), pltpu.VMEM((SUBLANES, c), F32)],
        compiler_params=_cparams(("parallel", "arbitrary")),
        name="lru",
    )(lx, lg, cw, cb, wa, ba, wx, bx, lam, og)


def _split_bf16(x):
    hi = x.astype(BF16)
    lo = (x - hi.astype(F32)).astype(BF16)
    return hi, lo


def _sb_attn_kernel(q_ref, k_ref, v_ref, o_ref, acc_ref, run_ref, *, tq, tk, slabs):
    qi = pl.program_id(2)
    lane = lax.broadcasted_iota(I32, (1, LANES), 1)
    head_masks = [lane < HEAD_DIM, lane >= HEAD_DIM]
    zero = jnp.zeros((), BF16)
    qs = []
    for sl in range(slabs):
        q = q_ref[0, :, sl * LANES:(sl + 1) * LANES] * jnp.asarray(HEAD_DIM ** -0.5, BF16)
        qs.append([jnp.where(m, q, zero) for m in head_masks])

    r = lax.broadcasted_iota(I32, (tk, tk), 0)
    c = lax.broadcasted_iota(I32, (tk, tk), 1)
    suffix = jnp.where(r > c, 1.0, 0.0).astype(BF16)

    acc_ref[...] = jnp.zeros_like(acc_ref)
    run_ref[...] = jnp.zeros_like(run_ref)

    def tile(j, diagonal):
        rows = pl.ds(pl.multiple_of(j * tk, tk), tk)
        if diagonal:
            qpos = lax.broadcasted_iota(I32, (tq, tk), 0)
            kpos = lax.broadcasted_iota(I32, (tq, tk), 1)
            visible = kpos < qpos
        for sl in range(slabs):
            ks = k_ref[0, rows, sl * LANES:(sl + 1) * LANES]
            vs = v_ref[0, rows, sl * LANES:(sl + 1) * LANES]
            pv = None
            for hd in range(2):
                z = lax.dot_general(qs[sl][hd], ks, (((1,), (1,)), ((), ())), preferred_element_type=F32)
                drop = jnp.maximum(z, 0.0) + jnp.log(1.0 + jnp.exp(-jnp.abs(z)))
                if diagonal:
                    drop = jnp.where(visible, drop, 0.0)
                sums = jnp.dot(drop.astype(BF16), suffix, preferred_element_type=F32)
                run = run_ref[2 * sl + hd]
                w = jnp.exp(z - ((drop + sums) + run))
                if diagonal:
                    w = jnp.where(visible, w, 0.0)
                vh = jnp.where(head_masks[hd], vs, zero)
                part = jnp.dot(w.astype(BF16), vh, preferred_element_type=F32)
                pv = part if pv is None else pv + part
                run_ref[2 * sl + hd] = run + (sums[:, 0:1] + drop[:, 0:1])
            acc_ref[:, sl * LANES:(sl + 1) * LANES] += pv

    tile(qi, True)

    def body(i, carry):
        tile(qi - 1 - i, False)
        return carry

    lax.fori_loop(0, qi, body, 0)
    o_ref[0] = acc_ref[...]


def _sb_attn(q, k, v, tq=512, slabs=2):
    b, s, c = q.shape
    tk = tq
    w = LANES * slabs
    qspec = pl.BlockSpec((1, tq, w), lambda i, j, l: (i, l, j))
    kvspec = pl.BlockSpec((1, s, w), lambda i, j, l: (i, 0, j))
    return pl.pallas_call(
        functools.partial(_sb_attn_kernel, tq=tq, tk=tk, slabs=slabs),
        grid=(b, c // w, s // tq),
        in_specs=[qspec, kvspec, kvspec],
        out_specs=qspec,
        out_shape=jax.ShapeDtypeStruct((b, s, c), F32),
        scratch_shapes=[pltpu.VMEM((tq, w), F32), pltpu.VMEM((2 * slabs, tq, 1), F32)],
        compiler_params=_cparams(("parallel", "parallel", "arbitrary")),
        name="sb_attn",
    )(q, k, v)


def _out_route_kernel(x_ref, lru_ref, sb_ref, sbg_ref, wol_ref, wos_ref, fg_ref, wr_ref, br_ref,
                      h_ref, xn3_ref, eidx_ref, gate_ref, rank_ref, cnt_ref, tbase_ref, cnt_scr):
    tm = x_ref.shape[0]

    @pl.when(pl.program_id(0) == 0)
    def _():
        cnt_scr[...] = jnp.zeros_like(cnt_scr)

    sbn = _rms(sb_ref[...], sbg_ref[...]).astype(BF16)
    h = (x_ref[...] + jnp.dot(lru_ref[...], wol_ref[...], preferred_element_type=F32)
         + jnp.dot(sbn, wos_ref[...], preferred_element_type=F32))
    h_ref[...] = h
    xn = _rms(h, fg_ref[...])
    for j in range(xn.shape[1] // LANES):
        xn3_ref[pl.ds(j, tm, stride=SUBLANES), :] = xn[:, j * LANES:(j + 1) * LANES]

    nt = (((1,), (1,)), ((), ()))
    w_hi, w_lo = _split_bf16(wr_ref[...])
    x_hi, x_lo = _split_bf16(xn)
    logits = (lax.dot_general(w_hi, x_hi, nt, preferred_element_type=F32)
              + (lax.dot_general(w_hi, x_lo, nt, preferred_element_type=F32)
                 + lax.dot_general(w_lo, x_hi, nt, preferred_element_type=F32))) + br_ref[...]
    n_exp = logits.shape[0]
    eio = lax.broadcasted_iota(I32, logits.shape, 0)
    work = logits
    vals, hits, idxs = [], [], []
    for _k in range(TOP_K):
        m = jnp.max(work, axis=0, keepdims=True)
        idx = jnp.min(jnp.where(work == m, eio, n_exp), axis=0, keepdims=True)
        hit = eio == idx
        work = jnp.where(hit, -jnp.inf, work)
        vals.append(m)
        hits.append(hit)
        idxs.append(idx)
    exps = [jnp.exp(v - vals[0]) for v in vals]
    denom = exps[0] + exps[1] + exps[2] + exps[3]
    gates = [e / denom for e in exps]

    onehot = jnp.where(hits[0] | hits[1] | hits[2] | hits[3], 1.0, 0.0).astype(BF16)
    r = lax.broadcasted_iota(I32, (tm, tm + LANES), 0)
    c = lax.broadcasted_iota(I32, (tm, tm + LANES), 1)
    prefix_mat = jnp.where((r < c) | (c >= tm), 1.0, 0.0).astype(BF16)
    sums = jnp.dot(onehot, prefix_mat, preferred_element_type=F32)
    base = cnt_scr[...]
    pos = sums[:, :tm] + jnp.concatenate([base] * (tm // LANES), axis=1)
    ranks = [jnp.sum(jnp.where(hk, pos, 0.0), axis=0, keepdims=True) for hk in hits]
    tbase_ref[...] = base
    cnt_scr[...] = base + sums[:, tm:]
    cnt_ref[...] = cnt_scr[...]

    pad_i = jnp.zeros((SUBLANES - TOP_K, tm), I32)
    eidx_ref[...] = jnp.concatenate(idxs + [pad_i], axis=0)
    rank_ref[...] = jnp.concatenate([rk.astype(I32) for rk in ranks] + [pad_i], axis=0)
    gate_ref[...] = jnp.concatenate(gates + [jnp.zeros((SUBLANES - TOP_K, tm), F32)], axis=0)


def _out_route(x2, lru_n, sb_y, sbg, wol, wos, fg, wr_t, br, tm=ROUTE_TILE):
    t, d = x2.shape
    e = wr_t.shape[0]
    row = lambda i: (i, 0)
    col = lambda i: (0, i)
    const = lambda i: (0, 0)
    meta = pl.BlockSpec((SUBLANES, tm), col)
    return pl.pallas_call(
        _out_route_kernel,
        grid=(t // tm,),
        in_specs=[pl.BlockSpec((tm, d), row), pl.BlockSpec((tm, D_LRU), row), pl.BlockSpec((tm, D_SB), row),
                  pl.BlockSpec((1, D_SB), const), pl.BlockSpec((D_LRU, d), const), pl.BlockSpec((D_SB, d), const),
                  pl.BlockSpec((1, d), const), pl.BlockSpec((e, d), const), pl.BlockSpec((e, 1), const)],
        out_specs=[pl.BlockSpec((tm, d), row), pl.BlockSpec((tm * SUBLANES, LANES), row), meta, meta, meta,
                   pl.BlockSpec((e, LANES), const), pl.BlockSpec((e, LANES), row)],
        out_shape=[jax.ShapeDtypeStruct((t, d), F32), jax.ShapeDtypeStruct((t * SUBLANES, LANES), F32),
                   jax.ShapeDtypeStruct((SUBLANES, t), I32), jax.ShapeDtypeStruct((SUBLANES, t), F32),
                   jax.ShapeDtypeStruct((SUBLANES, t), I32), jax.ShapeDtypeStruct((e, LANES), F32),
                   jax.ShapeDtypeStruct((t // tm * e, LANES), F32)],
        scratch_shapes=[pltpu.VMEM((e, LANES), F32)],
        compiler_params=_cparams(("arbitrary",)),
        name="out_route",
    )(x2, lru_n, sb_y, sbg, wol, wos, fg, wr_t, br)


def _row_tile(ref, row):
    return ref.at[pl.ds(pl.multiple_of(row * SUBLANES, SUBLANES), SUBLANES)]


def _dispatch_kernel(pstart_ref, cnt_ref, rstart_ref, rcnt_ref, tbase_ref, eidx_ref, rank_ref, xn3_ref, xs_out,
                     li_vmem, li_smem, delta_smem, npend_smem, stage, zero_tile,
                     sem_idx, sem_chunk, sem_row, sem_pad, sem_blk, *, n_steps):
    tm = eidx_ref.shape[1]

    @pl.when(pl.program_id(0) == 0)
    def _():
        zero_tile[...] = jnp.zeros_like(zero_tile)

        def pad_copy(row):
            return pltpu.make_async_copy(zero_tile.at[pl.ds(0, SUBLANES)], _row_tile(xs_out, row), sem_pad)

        def pads_of(ex, action):
            first_pad = pstart_ref[ex] + cnt_ref[ex]
            n_pad = (0 - cnt_ref[ex]) & (ROW_BLOCK - 1)

            def one(r, c):
                action(pad_copy(first_pad + r))
                return c

            lax.fori_loop(0, n_pad, one, 0)

        def fill_expert(ex, carry):
            pads_of(ex, lambda cp: cp.start())
            return carry

        def drain_expert(ex, carry):
            pads_of(ex, lambda cp: cp.wait())
            return carry

        lax.fori_loop(0, N_EXPERTS, fill_expert, 0)

        last = N_EXPERTS - 1
        used_rows = pstart_ref[last] + cnt_ref[last] + ((0 - cnt_ref[last]) & (ROW_BLOCK - 1))
        block_rows = ROW_BLOCK * SUBLANES

        def block_copy(blk):
            return pltpu.make_async_copy(
                zero_tile, xs_out.at[pl.ds(pl.multiple_of(blk * block_rows, block_rows), block_rows)], sem_blk)

        def fill_block(blk, c):
            block_copy(blk).start()
            return c

        def drain_block(blk, c):
            block_copy(blk).wait()
            return c

        first_free = used_rows // ROW_BLOCK
        lax.fori_loop(first_free, xs_out.shape[0] // block_rows, fill_block, 0)
        lax.fori_loop(0, N_EXPERTS, drain_expert, 0)
        lax.fori_loop(first_free, xs_out.shape[0] // block_rows, drain_block, 0)

    i = pl.program_id(0)
    slot = lax.rem(i, 2)
    chunk_rows = RUN_CHUNK * SUBLANES
    slot_rows = stage.shape[0] // 2

    def chunk_copy(src, dst, sl):
        return pltpu.make_async_copy(stage.at[pl.ds(pl.multiple_of(src, SUBLANES), chunk_rows)],
                                     xs_out.at[pl.ds(pl.multiple_of(dst, SUBLANES), chunk_rows)], sem_chunk.at[sl])

    def row_copy(src, dst, sl):
        return pltpu.make_async_copy(stage.at[pl.ds(pl.multiple_of(src, SUBLANES), SUBLANES)],
                                     xs_out.at[pl.ds(pl.multiple_of(dst, SUBLANES), SUBLANES)], sem_row.at[sl])

    def drain(sl):
        def wait_chunk(c, carry):
            chunk_copy(sl * slot_rows, 0, sl).wait()
            return carry

        def wait_row(c, carry):
            row_copy(sl * slot_rows, 0, sl).wait()
            return carry

        lax.fori_loop(0, npend_smem[2 * sl], wait_chunk, 0)
        lax.fori_loop(0, npend_smem[2 * sl + 1], wait_row, 0)

    @pl.when(i >= 2)
    def _():
        drain(slot)

    def fill_delta(ex, off):
        delta_smem[ex] = off - tbase_ref[i * N_EXPERTS + ex]
        return off + rcnt_ref[i * N_EXPERTS + ex]

    lax.fori_loop(0, N_EXPERTS, fill_delta, 0)
    e = eidx_ref[...]
    delta = jnp.zeros_like(e)
    for ex in range(N_EXPERTS):
        delta = jnp.where(e == ex, delta_smem[ex], delta)
    li_vmem[...] = (rank_ref[...] + delta) * SUBLANES + slot * slot_rows
    to_smem = pltpu.make_async_copy(li_vmem, li_smem, sem_idx)
    to_smem.start()
    to_smem.wait()

    for tok in range(tm):
        row = xn3_ref[tok * SUBLANES:(tok + 1) * SUBLANES, :]
        for k in range(TOP_K):
            stage[pl.ds(pl.multiple_of(li_smem[k, tok], SUBLANES), SUBLANES), :] = row

    def send_expert(ex, carry):
        src, n_chunks, n_rows = carry
        c = rcnt_ref[i * N_EXPERTS + ex]
        dst = rstart_ref[i * N_EXPERTS + ex] * SUBLANES
        n_full = c >> RUN_SHIFT
        n_rem = c & (RUN_CHUNK - 1)

        def one_chunk(ch, cc):
            chunk_copy(src + ch * chunk_rows, dst + ch * chunk_rows, slot).start()
            return cc

        def one_row(r, cc):
            o = n_full * chunk_rows + r * SUBLANES
            row_copy(src + o, dst + o, slot).start()
            return cc

        lax.fori_loop(0, n_full, one_chunk, 0)
        lax.fori_loop(0, n_rem, one_row, 0)
        return src + c * SUBLANES, n_chunks + n_full, n_rows + n_rem

    _, n_chunks, n_rows = lax.fori_loop(0, N_EXPERTS, send_expert, (slot * slot_rows, 0, 0))
    npend_smem[2 * slot] = n_chunks
    npend_smem[2 * slot + 1] = n_rows

    @pl.when(i == n_steps - 1)
    def _():
        drain(slot)
        if n_steps >= 2:
            drain(1 - slot)


def _dispatch(pstart, cnt, rstart, rcnt, tbase, eidx, ranks, xn3, n_rows, tm=ROUTE_TILE):
    t = eidx.shape[1]
    meta = pl.BlockSpec((SUBLANES, tm), lambda i, *_: (0, i))
    grid_spec = pltpu.PrefetchScalarGridSpec(
        num_scalar_prefetch=5,
        grid=(t // tm,),
        in_specs=[meta, meta, pl.BlockSpec((tm * SUBLANES, LANES), lambda i, *_: (i, 0))],
        out_specs=pl.BlockSpec(memory_space=pl.ANY),
        scratch_shapes=[pltpu.VMEM((SUBLANES, tm), I32), pltpu.SMEM((SUBLANES, tm), I32),
                        pltpu.SMEM((N_EXPERTS,), I32), pltpu.SMEM((4,), I32),
                        pltpu.VMEM((2 * TOP_K * tm * SUBLANES, LANES), F32),
                        pltpu.VMEM((ROW_BLOCK * SUBLANES, LANES), F32),
                        pltpu.SemaphoreType.DMA, pltpu.SemaphoreType.DMA((2,)), pltpu.SemaphoreType.DMA((2,)),
                        pltpu.SemaphoreType.DMA, pltpu.SemaphoreType.DMA],
    )
    return pl.pallas_call(
        functools.partial(_dispatch_kernel, n_steps=t // tm),
        grid_spec=grid_spec,
        out_shape=jax.ShapeDtypeStruct((n_rows * SUBLANES, LANES), F32),
        compiler_params=_cparams(("arbitrary",)),
        name="dispatch",
    )(pstart, cnt, rstart, rcnt, tbase, eidx, ranks, xn3)


def _experts_kernel(bexp_ref, nblk_ref, nbe_ref, xs_ref, bup_ref, bdn_ref, wup_hbm, wdn_hbm, y_ref,
                    wup32, wdn32, wup16, wdn16, x16, sem_up, sem_dn):
    i = pl.program_id(0)
    rows = x16.shape[0]
    d = x16.shape[1]
    de = wdn16.shape[0]
    e = bexp_ref[i]
    active = i < nblk_ref[0]
    first = (i == 0) | (e != bexp_ref[jnp.maximum(i - 1, 0)])

    def fetch(ex):
        return (pltpu.make_async_copy(wup_hbm.at[ex], wup32, sem_up),
                pltpu.make_async_copy(wdn_hbm.at[ex], wdn32, sem_dn))

    @pl.when(i == 0)
    def _():
        for cp in fetch(e):
            cp.start()

    @pl.when(active & first)
    def _():
        for cp in fetch(e):
            cp.wait()
        wup16[...] = wup32[...].astype(BF16)
        wdn16[...] = wdn32[...].astype(BF16)
        nxt = i + nbe_ref[e]

        @pl.when(nxt < nblk_ref[0])
        def _():
            for cp in fetch(bexp_ref[nxt]):
                cp.start()

    @pl.when(active)
    def _():
        for j in range(d // LANES):
            x16[:, j * LANES:(j + 1) * LANES] = xs_ref[pl.ds(j, rows, stride=SUBLANES), :].astype(BF16)
        hdn = jnp.dot(x16[...], wup16[...], preferred_element_type=F32) + bup_ref[0]
        g = jnp.minimum(hdn[:, :de], SWIGLU_LIMIT)
        u = jnp.clip(hdn[:, de:], -SWIGLU_LIMIT, SWIGLU_LIMIT)
        glu = g * jax.nn.sigmoid(SWIGLU_ALPHA * g)
        act = ((u + 1.0) * glu).astype(BF16)
        y = jnp.dot(act, wdn16[...], preferred_element_type=F32) + bdn_ref[0]
        for j in range(d // LANES):
            y_ref[pl.ds(j, rows, stride=SUBLANES), :] = y[:, j * LANES:(j + 1) * LANES]

    @pl.when(jnp.logical_not(active))
    def _():
        y_ref[...] = jnp.zeros_like(y_ref)


def _experts(bexp, nblk, nbe, xs, w_up, b_up, w_down, b_down):
    n_exp, d, d2 = w_up.shape
    de = w_down.shape[1]
    n_blocks = bexp.shape[0]
    rows = ROW_BLOCK
    xspec = pl.BlockSpec((rows * SUBLANES, LANES), lambda i, be, nb, ne: (i, 0))
    any_spec = pl.BlockSpec(memory_space=pl.ANY)
    grid_spec = pltpu.PrefetchScalarGridSpec(
        num_scalar_prefetch=3,
        grid=(n_blocks,),
        in_specs=[xspec,
                  pl.BlockSpec((1, 1, d2), lambda i, be, nb, ne: (be[i], 0, 0)),
                  pl.BlockSpec((1, 1, d), lambda i, be, nb, ne: (be[i], 0, 0)),
                  any_spec, any_spec],
        out_specs=xspec,
        scratch_shapes=[pltpu.VMEM((d, d2), F32), pltpu.VMEM((de, d), F32),
                        pltpu.VMEM((d, d2), BF16), pltpu.VMEM((de, d), BF16), pltpu.VMEM((rows, d), BF16),
                        pltpu.SemaphoreType.DMA, pltpu.SemaphoreType.DMA],
    )
    return pl.pallas_call(
        _experts_kernel,
        grid_spec=grid_spec,
        out_shape=jax.ShapeDtypeStruct(xs.shape, F32),
        compiler_params=_cparams(("arbitrary",)),
        name="experts",
    )(bexp, nblk, nbe, xs, b_up[:, None, :], b_down[:, None, :], w_up, w_down)


def _combine_kernel(rstart_ref, rcnt_ref, tbase_ref, eidx_ref, rank_ref, gate_ref, h_ref, p_ref, pg_ref, wpg_ref,
                    wp_ref, fg_ref, y_hbm, out_ref,
                    stage, li_vmem, li_smem, gate_smem, delta_smem, nchunk_smem, moe_rows, sem_idx, sem_runs):
    tm = h_ref.shape[0]
    d = h_ref.shape[1]
    i = pl.program_id(0)
    n = pl.num_programs(0)
    slot = lax.rem(i, 2)
    chunk_rows = RUN_CHUNK * SUBLANES
    slot_rows = stage.shape[0] // 2

    def chunks_of(c):
        return (c + (RUN_CHUNK - 1)) >> RUN_SHIFT

    def chunk_copy(src, dst, sl):
        return pltpu.make_async_copy(y_hbm.at[pl.ds(pl.multiple_of(src, SUBLANES), chunk_rows)],
                                     stage.at[pl.ds(pl.multiple_of(dst, SUBLANES), chunk_rows)], sem_runs.at[sl])

    def start_runs(tile, sl):
        def per_expert(ex, dst):
            src = rstart_ref[tile * N_EXPERTS + ex] * SUBLANES
            n_ch = chunks_of(rcnt_ref[tile * N_EXPERTS + ex])

            def one(ch, carry):
                chunk_copy(src + ch * chunk_rows, dst + ch * chunk_rows, sl).start()
                return carry

            lax.fori_loop(0, n_ch, one, 0)
            return dst + n_ch * chunk_rows

        end = lax.fori_loop(0, N_EXPERTS, per_expert, sl * slot_rows)
        nchunk_smem[sl] = (end - sl * slot_rows) >> (RUN_SHIFT + 3)

    @pl.when(i == 0)
    def _():
        start_runs(0, 0)

    @pl.when(i + 1 < n)
    def _():
        start_runs(i + 1, 1 - slot)

    def fill_delta(ex, off):
        delta_smem[ex] = off - tbase_ref[i * N_EXPERTS + ex]
        return off + chunks_of(rcnt_ref[i * N_EXPERTS + ex]) * RUN_CHUNK

    lax.fori_loop(0, N_EXPERTS, fill_delta, 0)
    e = eidx_ref[...]
    delta = jnp.zeros_like(e)
    for ex in range(N_EXPERTS):
        delta = jnp.where(e == ex, delta_smem[ex], delta)
    li_vmem[...] = (rank_ref[...] + delta) * SUBLANES + slot * slot_rows
    to_smem = [pltpu.make_async_copy(li_vmem, li_smem, sem_idx.at[0]),
               pltpu.make_async_copy(gate_ref, gate_smem, sem_idx.at[1])]
    for cp in to_smem:
        cp.start()
    for cp in to_smem:
        cp.wait()

    def wait_chunk(ch, carry):
        chunk_copy(0, slot * slot_rows, slot).wait()
        return carry

    lax.fori_loop(0, nchunk_smem[slot], wait_chunk, 0)

    for tok in range(tm):
        acc = None
        for k in range(TOP_K):
            row = stage[pl.ds(pl.multiple_of(li_smem[k, tok], SUBLANES), SUBLANES), :]
            term = gate_smem[k, tok] * row
            acc = term if acc is None else acc + term
        moe_rows[tok * SUBLANES:(tok + 1) * SUBLANES, :] = acc

    moe = jnp.concatenate([moe_rows[pl.ds(j, tm, stride=SUBLANES), :] for j in range(d // LANES)], axis=1)
    h = h_ref[...] + moe

    ple = jnp.dot(p_ref[...].astype(BF16), wp_ref[...], preferred_element_type=F32)
    ple_gate = jax.nn.sigmoid(jnp.dot(_rms(h, pg_ref[...]).astype(BF16), wpg_ref[...],
                                      preferred_element_type=F32))
    h = h + ple * ple_gate
    out_ref[...] = _rms(h, fg_ref[...])


def _combine(rstart, rcnt, tbase, eidx, ranks, gates, h1, p2, pg, wpg, wp, fg, y, tm=ROUTE_TILE):
    t, d = h1.shape
    dp = p2.shape[1]
    row = lambda i, *_: (i, 0)
    col = lambda i, *_: (0, i)
    const = lambda i, *_: (0, 0)
    meta = pl.BlockSpec((SUBLANES, tm), col)
    stage_rows = (TOP_K * tm + N_EXPERTS * RUN_CHUNK) * SUBLANES
    grid_spec = pltpu.PrefetchScalarGridSpec(
        num_scalar_prefetch=3,
        grid=(t // tm,),
        in_specs=[meta, meta, meta, pl.BlockSpec((tm, d), row), pl.BlockSpec((tm, dp), row),
                  pl.BlockSpec((1, d), const), pl.BlockSpec((d, d), const), pl.BlockSpec((dp, d), const),
                  pl.BlockSpec((1, d), const), pl.BlockSpec(memory_space=pl.ANY)],
        out_specs=pl.BlockSpec((tm, d), row),
        scratch_shapes=[pltpu.VMEM((2 * stage_rows, LANES), F32), pltpu.VMEM((SUBLANES, tm), I32),
                        pltpu.SMEM((SUBLANES, tm), I32), pltpu.SMEM((SUBLANES, tm), F32),
                        pltpu.SMEM((N_EXPERTS,), I32), pltpu.SMEM((2,), I32),
                        pltpu.VMEM((tm * SUBLANES, LANES), F32),
                        pltpu.SemaphoreType.DMA((2,)), pltpu.SemaphoreType.DMA((2,))],
    )
    return pl.pallas_call(
        _combine_kernel,
        grid_spec=grid_spec,
        out_shape=jax.ShapeDtypeStruct((t, d), F32),
        compiler_params=_cparams(("arbitrary",)),
        name="combine",
    )(rstart, rcnt, tbase, eidx, ranks, gates, h1, p2, pg, wpg, wp, fg, y)


def _block_diag(w):
    n, d, _ = w.shape
    eye = jnp.eye(n, dtype=w.dtype)
    return (eye[:, None, :, None] * w[:, :, None, :]).reshape(n * d, n * d)


def _stages(x, p, mix_norm_g, w_in, conv_w, conv_b, lru_w_a, lru_b_a, lru_w_x, lru_b_x, lru_lambda,
            lru_out_g, sb_out_g, w_out, ffn_norm_g, w_router, b_router, w_up, b_up, w_down, b_down,
            ple_norm_g, w_ple_gate, w_ple, final_norm_g):
    b, s, d = x.shape
    t = b * s
    st = {}
    x2 = x.reshape(t, d)
    lx, lg, q, k, v = _in_proj(x2, mix_norm_g[0][None], w_in[0].astype(BF16))
    st.update(lru_x=lx, lru_gate=lg, q=q, k=k, v=v)
    row = lambda a: a[None].astype(F32)
    lru_n = _lru(lx.reshape(b, s, D_LRU), lg.reshape(b, s, D_LRU), conv_w[0], row(conv_b[0]),
                 _block_diag(lru_w_a[0]).astype(BF16), row(lru_b_a[0]),
                 _block_diag(lru_w_x[0]).astype(BF16), row(lru_b_x[0]),
                 row(lru_lambda[0]), row(lru_out_g[0]))
    st["lru_n"] = lru_n
    sb_y = _sb_attn(q.reshape(b, s, D_SB), k.reshape(b, s, D_SB), v.reshape(b, s, D_SB))
    st["sb_y"] = sb_y
    wo = w_out[0].astype(BF16)
    h1, xn3, eidx, gates, ranks, counts, tile_base = _out_route(
        x2, lru_n.reshape(t, D_LRU), sb_y.reshape(t, D_SB), row(sb_out_g[0]), wo[:D_LRU], wo[D_LRU:],
        row(ffn_norm_g[0]), w_router[0].T, b_router[0][:, None])
    st.update(h1=h1, xn1=xn3, eidx=eidx, gates=gates, ranks=ranks, counts=counts)

    cnt = counts[:, 0].astype(I32)
    padded = (cnt + ROW_BLOCK - 1) // ROW_BLOCK * ROW_BLOCK
    pend = jnp.cumsum(padded)
    pstart = pend - padded
    n_blocks = -(-(t * TOP_K) // ROW_BLOCK) + N_EXPERTS
    block_row0 = jnp.arange(n_blocks, dtype=I32) * ROW_BLOCK
    bexp = jnp.minimum(jnp.sum((pend[None, :] <= block_row0[:, None]).astype(I32), axis=1), N_EXPERTS - 1)
    nblk = (pend[-1:] // ROW_BLOCK).astype(I32)

    tbase = tile_base[:, 0].astype(I32).reshape(-1, N_EXPERTS)
    rcnt = jnp.concatenate([tbase[1:], cnt[None, :]], axis=0) - tbase
    rstart = pstart[None, :] + tbase

    xs = _dispatch(pstart, cnt, rstart.reshape(-1), rcnt.reshape(-1), tbase.reshape(-1), eidx, ranks, xn3,
                   n_blocks * ROW_BLOCK)
    y = _experts(bexp, nblk, padded // ROW_BLOCK, xs, w_up[0], b_up[0], w_down[0], b_down[0])
    st.update(xs=xs, y=y)
    out = _combine(rstart.reshape(-1), rcnt.reshape(-1), tbase.reshape(-1), eidx, ranks, gates, h1,
                   p[0].reshape(t, -1), row(ple_norm_g[0]), w_ple_gate[0].astype(BF16),
                   w_ple[0].astype(BF16), row(final_norm_g), y)
    st["final"] = out.reshape(b, s, d)
    return st


def kernel(x, p, mix_norm_g, w_in, conv_w, conv_b, lru_w_a, lru_b_a, lru_w_x, lru_b_x, lru_lambda,
           lru_out_g, sb_out_g, w_out, ffn_norm_g, w_router, b_router, w_up, b_up, w_down, b_down,
           ple_norm_g, w_ple_gate, w_ple, final_norm_g):
    return _stages(x, p, mix_norm_g, w_in, conv_w, conv_b, lru_w_a, lru_b_a, lru_w_x, lru_b_x, lru_lambda,
                   lru_out_g, sb_out_g, w_out, ffn_norm_g, w_router, b_router, w_up, b_up, w_down, b_down,
                   ple_norm_g, w_ple_gate, w_ple, final_norm_g)["final"]
```

```python
import functools

import jax
import jax.numpy as jnp
from jax import lax
from jax.experimental import pallas as pl
from jax.experimental.pallas import tpu as pltpu

F32 = jnp.float32
BF16 = jnp.bfloat16
I32 = jnp.int32

RMS_EPS = 1e-6
LANES = 128
SUBLANES = 8
D_LRU = 512
D_SB = 512
N_HEADS = 8
HEAD_DIM = 64
CONV_WIDTH = 4
LRU_C = 8.0
N_EXPERTS = 32
TOP_K = 4
SWIGLU_LIMIT = 7.0
SWIGLU_ALPHA = 1.702
ROW_BLOCK = 256
ROUTE_TILE = 512
RUN_SHIFT = 5
RUN_CHUNK = 1 << RUN_SHIFT
VMEM_LIMIT = 56 * 1024 * 1024


def _rms(x, g):
    return (x * lax.rsqrt(jnp.mean(x * x, axis=-1, keepdims=True) + RMS_EPS)) * g


def _cparams(sem, flags=None):
    return pltpu.CompilerParams(dimension_semantics=sem, vmem_limit_bytes=VMEM_LIMIT, flags=flags)


def _in_proj_kernel(x_ref, g_ref, w_ref, lx_ref, lg_ref, q_ref, k_ref, v_ref):
    xn = _rms(x_ref[...], g_ref[...])
    proj = jnp.dot(xn.astype(BF16), w_ref[...], preferred_element_type=F32)
    lx_ref[...] = proj[:, 0:D_LRU]
    lg_ref[...] = proj[:, D_LRU:2 * D_LRU]
    o = 2 * D_LRU
    q_ref[...] = proj[:, o:o + D_SB].astype(BF16)
    k_ref[...] = proj[:, o + D_SB:o + 2 * D_SB].astype(BF16)
    v_ref[...] = proj[:, o + 2 * D_SB:o + 3 * D_SB].astype(BF16)


def _in_proj(x2, g, w_bf, tm=512):
    t, d = x2.shape
    n = w_bf.shape[1]
    row = lambda i: (i, 0)
    const = lambda i: (0, 0)
    return pl.pallas_call(
        _in_proj_kernel,
        grid=(t // tm,),
        in_specs=[pl.BlockSpec((tm, d), row), pl.BlockSpec((1, d), const), pl.BlockSpec((d, n), const)],
        out_specs=[pl.BlockSpec((tm, D_LRU), row)] * 2 + [pl.BlockSpec((tm, D_SB), row)] * 3,
        out_shape=[jax.ShapeDtypeStruct((t, D_LRU), F32)] * 2 + [jax.ShapeDtypeStruct((t, D_SB), BF16)] * 3,
        compiler_params=_cparams(("parallel",)),
        name="in_proj",
    )(x2, g, w_bf)


def _shift_rows(x, k, fill):
    if k % SUBLANES == 0:
        return jnp.concatenate([jnp.full((k, x.shape[1]), fill, x.dtype), x[:x.shape[0] - k]], axis=0)
    rolled = pltpu.roll(x, k, 0)
    rows = lax.broadcasted_iota(I32, x.shape, 0)
    return jnp.where(rows >= k, rolled, fill)


def _lru_kernel(lx_ref, lg_ref, cw_ref, cb_ref, wa_ref, ba_ref, wx_ref, bx_ref, lam_ref, og_ref,
                out_ref, tail_ref, h_ref):
    ts = lx_ref.shape[1]

    @pl.when(pl.program_id(1) == 0)
    def _():
        tail_ref[...] = jnp.zeros_like(tail_ref)
        h_ref[...] = jnp.zeros_like(h_ref)

    x = lx_ref[0]
    tail = tail_ref[...]
    rows = lax.broadcasted_iota(I32, x.shape, 0)
    cw = cw_ref[...]
    conv = x * cw[CONV_WIDTH - 1:CONV_WIDTH, :] + cb_ref[...]
    for k in range(1, CONV_WIDTH):
        cur = pltpu.roll(x, k, 0)
        prev = pltpu.roll(tail, k, 0)
        prev_full = jnp.concatenate([prev] + [prev] * (ts // SUBLANES - 1), axis=0)
        shifted = jnp.where(rows >= k, cur, prev_full)
        conv = conv + shifted * cw[CONV_WIDTH - 1 - k:CONV_WIDTH - k, :]
    tail_ref[...] = x[ts - SUBLANES:, :]

    cb16 = conv.astype(BF16)
    r = jax.nn.sigmoid(jnp.dot(cb16, wa_ref[...], preferred_element_type=F32) + ba_ref[...])
    gi = jax.nn.sigmoid(jnp.dot(cb16, wx_ref[...], preferred_element_type=F32) + bx_ref[...])
    lam = lam_ref[...]
    softplus_neg = jnp.maximum(-lam, 0.0) + jnp.log1p(jnp.exp(-jnp.abs(lam)))
    log_a = (-LRU_C * r) * softplus_neg
    a = jnp.exp(log_a)
    b = jnp.sqrt(1.0 - jnp.exp(2.0 * log_a)) * (gi * conv)

    k = 1
    while k < ts:
        a_sh = _shift_rows(a, k, 1.0)
        b_sh = _shift_rows(b, k, 0.0)
        b = a * b_sh + b
        a = a * a_sh
        k *= 2
    h = a * h_ref[0:1, :] + b
    h_ref[...] = jnp.broadcast_to(h[ts - 1:ts, :], h_ref.shape)

    gate = lg_ref[0]
    y = h * jax.nn.gelu(gate)
    out_ref[0] = _rms(y, og_ref[...]).astype(out_ref.dtype)


def _lru(lx, lg, cw, cb, wa, ba, wx, bx, lam, og, ts=256):
    b, s, c = lx.shape
    tile = lambda i, j: (i, j, 0)
    const = lambda i, j: (0, 0)
    vec = pl.BlockSpec((1, c), const)
    return pl.pallas_call(
        _lru_kernel,
        grid=(b, s // ts),
        in_specs=[pl.BlockSpec((1, ts, c), tile), pl.BlockSpec((1, ts, c), tile),
                  pl.BlockSpec((CONV_WIDTH, c), const), vec,
                  pl.BlockSpec((c, c), const), vec, pl.BlockSpec((c, c), const), vec, vec, vec],
        out_specs=pl.BlockSpec((1, ts, c), tile),
        out_shape=jax.ShapeDtypeStruct((b, s, c), BF16),
        scratch_shapes=[pltpu.VMEM((SUBLANES, c), F32), pltpu.VMEM((SUBLANES, c), F32)],
        compiler_params=_cparams(("parallel", "arbitrary")),
        name="lru",
    )(lx, lg, cw, cb, wa, ba, wx, bx, lam, og)


def _split_bf16(x):
    hi = x.astype(BF16)
    lo = (x - hi.astype(F32)).astype(BF16)
    return hi, lo


def _sb_attn_kernel(q_ref, k_ref, v_ref, o_ref, acc_ref, run_ref, *, tq, tk, slabs):
    qi = pl.program_id(2)
    lane = lax.broadcasted_iota(I32, (1, LANES), 1)
    head_masks = [lane < HEAD_DIM, lane >= HEAD_DIM]
    zero = jnp.zeros((), BF16)
    qs = []
    for sl in range(slabs):
        q = q_ref[0, :, sl * LANES:(sl + 1) * LANES] * jnp.asarray(HEAD_DIM ** -0.5, BF16)
        qs.append([jnp.where(m, q, zero) for m in head_masks])

    r = lax.broadcasted_iota(I32, (tk, tk), 0)
    c = lax.broadcasted_iota(I32, (tk, tk), 1)
    suffix = jnp.where(r > c, 1.0, 0.0).astype(BF16)

    acc_ref[...] = jnp.zeros_like(acc_ref)
    run_ref[...] = jnp.zeros_like(run_ref)

    def tile(j, diagonal):
        rows = pl.ds(pl.multiple_of(j * tk, tk), tk)
        if diagonal:
            qpos = lax.broadcasted_iota(I32, (tq, tk), 0)
            kpos = lax.broadcasted_iota(I32, (tq, tk), 1)
            visible = kpos < qpos
        for sl in range(slabs):
            ks = k_ref[0, rows, sl * LANES:(sl + 1) * LANES]
            vs = v_ref[0, rows, sl * LANES:(sl + 1) * LANES]
            pv = None
            for hd in range(2):
                z = lax.dot_general(qs[sl][hd], ks, (((1,), (1,)), ((), ())), preferred_element_type=F32)
                drop = jnp.maximum(z, 0.0) + jnp.log(1.0 + jnp.exp(-jnp.abs(z)))
                if diagonal:
                    drop = jnp.where(visible, drop, 0.0)
                sums = jnp.dot(drop.astype(BF16), suffix, preferred_element_type=F32)
                run = run_ref[2 * sl + hd]
                w = jnp.exp(z - ((drop + sums) + run))
                if diagonal:
                    w = jnp.where(visible, w, 0.0)
                vh = jnp.where(head_masks[hd], vs, zero)
                part = jnp.dot(w.astype(BF16), vh, preferred_element_type=F32)
                pv = part if pv is None else pv + part
                run_ref[2 * sl + hd] = run + (sums[:, 0:1] + drop[:, 0:1])
            acc_ref[:, sl * LANES:(sl + 1) * LANES] += pv

    tile(qi, True)

    def body(i, carry):
        tile(qi - 1 - i, False)
        return carry

    lax.fori_loop(0, qi, body, 0)
    o_ref[0] = acc_ref[...]


def _sb_attn(q, k, v, tq=512, slabs=4):
    b, s, c = q.shape
    tk = tq
    w = LANES * slabs
    qspec = pl.BlockSpec((1, tq, w), lambda i, j, l: (i, l, j))
    kvspec = pl.BlockSpec((1, s, w), lambda i, j, l: (i, 0, j))
    return pl.pallas_call(
        functools.partial(_sb_attn_kernel, tq=tq, tk=tk, slabs=slabs),
        grid=(b, c // w, s // tq),
        in_specs=[qspec, kvspec, kvspec],
        out_specs=qspec,
        out_shape=jax.ShapeDtypeStruct((b, s, c), F32),
        scratch_shapes=[pltpu.VMEM((tq, w), F32), pltpu.VMEM((2 * slabs, tq, 1), F32)],
        compiler_params=_cparams(("parallel", "parallel", "arbitrary")),
        name="sb_attn",
    )(q, k, v)


def _out_route_kernel(x_ref, lru_ref, sb_ref, sbg_ref, wol_ref, wos_ref, fg_ref, wr_ref, br_ref,
                      h_ref, xn3_ref, eidx_ref, gate_ref, rank_ref, cnt_ref, tbase_ref, cnt_scr):
    tm = x_ref.shape[0]

    @pl.when(pl.program_id(0) == 0)
    def _():
        cnt_scr[...] = jnp.zeros_like(cnt_scr)

    sbn = _rms(sb_ref[...], sbg_ref[...]).astype(BF16)
    h = (x_ref[...] + jnp.dot(lru_ref[...], wol_ref[...], preferred_element_type=F32)
         + jnp.dot(sbn, wos_ref[...], preferred_element_type=F32))
    h_ref[...] = h
    xn = _rms(h, fg_ref[...])
    for j in range(xn.shape[1] // LANES):
        xn3_ref[pl.ds(j, tm, stride=SUBLANES), :] = xn[:, j * LANES:(j + 1) * LANES]

    nt = (((1,), (1,)), ((), ()))
    w_hi, w_lo = _split_bf16(wr_ref[...])
    x_hi, x_lo = _split_bf16(xn)
    logits = (lax.dot_general(w_hi, x_hi, nt, preferred_element_type=F32)
              + (lax.dot_general(w_hi, x_lo, nt, preferred_element_type=F32)
                 + lax.dot_general(w_lo, x_hi, nt, preferred_element_type=F32))) + br_ref[...]
    n_exp = logits.shape[0]
    eio = lax.broadcasted_iota(I32, logits.shape, 0)
    work = logits
    vals, hits, idxs = [], [], []
    for _k in range(TOP_K):
        m = jnp.max(work, axis=0, keepdims=True)
        idx = jnp.min(jnp.where(work == m, eio, n_exp), axis=0, keepdims=True)
        hit = eio == idx
        work = jnp.where(hit, -jnp.inf, work)
        vals.append(m)
        hits.append(hit)
        idxs.append(idx)
    exps = [jnp.exp(v - vals[0]) for v in vals]
    denom = exps[0] + exps[1] + exps[2] + exps[3]
    gates = [e / denom for e in exps]

    onehot = jnp.where(hits[0] | hits[1] | hits[2] | hits[3], 1.0, 0.0).astype(BF16)
    r = lax.broadcasted_iota(I32, (tm, tm + LANES), 0)
    c = lax.broadcasted_iota(I32, (tm, tm + LANES), 1)
    prefix_mat = jnp.where((r < c) | (c >= tm), 1.0, 0.0).astype(BF16)
    sums = jnp.dot(onehot, prefix_mat, preferred_element_type=F32)
    base = cnt_scr[...]
    pos = sums[:, :tm] + jnp.concatenate([base] * (tm // LANES), axis=1)
    ranks = [jnp.sum(jnp.where(hk, pos, 0.0), axis=0, keepdims=True) for hk in hits]
    tbase_ref[...] = base
    cnt_scr[...] = base + sums[:, tm:]
    cnt_ref[...] = cnt_scr[...]

    pad_i = jnp.zeros((SUBLANES - TOP_K, tm), I32)
    eidx_ref[...] = jnp.concatenate(idxs + [pad_i], axis=0)
    rank_ref[...] = jnp.concatenate([rk.astype(I32) for rk in ranks] + [pad_i], axis=0)
    gate_ref[...] = jnp.concatenate(gates + [jnp.zeros((SUBLANES - TOP_K, tm), F32)], axis=0)


def _out_route(x2, lru_n, sb_y, sbg, wol, wos, fg, wr_t, br, tm=ROUTE_TILE):
    t, d = x2.shape
    e = wr_t.shape[0]
    row = lambda i: (i, 0)
    col = lambda i: (0, i)
    const = lambda i: (0, 0)
    meta = pl.BlockSpec((SUBLANES, tm), col)
    return pl.pallas_call(
        _out_route_kernel,
        grid=(t // tm,),
        in_specs=[pl.BlockSpec((tm, d), row), pl.BlockSpec((tm, D_LRU), row), pl.BlockSpec((tm, D_SB), row),
                  pl.BlockSpec((1, D_SB), const), pl.BlockSpec((D_LRU, d), const), pl.BlockSpec((D_SB, d), const),
                  pl.BlockSpec((1, d), const), pl.BlockSpec((e, d), const), pl.BlockSpec((e, 1), const)],
        out_specs=[pl.BlockSpec((tm, d), row), pl.BlockSpec((tm * SUBLANES, LANES), row), meta, meta, meta,
                   pl.BlockSpec((e, LANES), const), pl.BlockSpec((e, LANES), row)],
        out_shape=[jax.ShapeDtypeStruct((t, d), F32), jax.ShapeDtypeStruct((t * SUBLANES, LANES), F32),
                   jax.ShapeDtypeStruct((SUBLANES, t), I32), jax.ShapeDtypeStruct((SUBLANES, t), F32),
                   jax.ShapeDtypeStruct((SUBLANES, t), I32), jax.ShapeDtypeStruct((e, LANES), F32),
                   jax.ShapeDtypeStruct((t // tm * e, LANES), F32)],
        scratch_shapes=[pltpu.VMEM((e, LANES), F32)],
        compiler_params=_cparams(("arbitrary",)),
        name="out_route",
    )(x2, lru_n, sb_y, sbg, wol, wos, fg, wr_t, br)


def _row_tile(ref, row):
    return ref.at[pl.ds(pl.multiple_of(row * SUBLANES, SUBLANES), SUBLANES)]


def _dispatch_kernel(pstart_ref, cnt_ref, rstart_ref, rcnt_ref, tbase_ref, eidx_ref, rank_ref, xn3_ref, xs_out,
                     li_vmem, li_smem, delta_smem, npend_smem, stage, zero_tile,
                     sem_idx, sem_chunk, sem_row, sem_pad, sem_blk, *, n_steps):
    tm = eidx_ref.shape[1]

    @pl.when(pl.program_id(0) == 0)
    def _():
        zero_tile[...] = jnp.zeros_like(zero_tile)

        def pad_copy(row):
            return pltpu.make_async_copy(zero_tile.at[pl.ds(0, SUBLANES)], _row_tile(xs_out, row), sem_pad)

        def pads_of(ex, action):
            first_pad = pstart_ref[ex] + cnt_ref[ex]
            n_pad = (0 - cnt_ref[ex]) & (ROW_BLOCK - 1)

            def one(r, c):
                action(pad_copy(first_pad + r))
                return c

            lax.fori_loop(0, n_pad, one, 0)

        def fill_expert(ex, carry):
            pads_of(ex, lambda cp: cp.start())
            return carry

        def drain_expert(ex, carry):
            pads_of(ex, lambda cp: cp.wait())
            return carry

        lax.fori_loop(0, N_EXPERTS, fill_expert, 0)

        last = N_EXPERTS - 1
        used_rows = pstart_ref[last] + cnt_ref[last] + ((0 - cnt_ref[last]) & (ROW_BLOCK - 1))
        block_rows = ROW_BLOCK * SUBLANES

        def block_copy(blk):
            return pltpu.make_async_copy(
                zero_tile, xs_out.at[pl.ds(pl.multiple_of(blk * block_rows, block_rows), block_rows)], sem_blk)

        def fill_block(blk, c):
            block_copy(blk).start()
            return c

        def drain_block(blk, c):
            block_copy(blk).wait()
            return c

        first_free = used_rows // ROW_BLOCK
        lax.fori_loop(first_free, xs_out.shape[0] // block_rows, fill_block, 0)
        lax.fori_loop(0, N_EXPERTS, drain_expert, 0)
        lax.fori_loop(first_free, xs_out.shape[0] // block_rows, drain_block, 0)

    i = pl.program_id(0)
    slot = lax.rem(i, 2)
    chunk_rows = RUN_CHUNK * SUBLANES
    slot_rows = stage.shape[0] // 2

    def chunk_copy(src, dst, sl):
        return pltpu.make_async_copy(stage.at[pl.ds(pl.multiple_of(src, SUBLANES), chunk_rows)],
                                     xs_out.at[pl.ds(pl.multiple_of(dst, SUBLANES), chunk_rows)], sem_chunk.at[sl])

    def row_copy(src, dst, sl):
        return pltpu.make_async_copy(stage.at[pl.ds(pl.multiple_of(src, SUBLANES), SUBLANES)],
                                     xs_out.at[pl.ds(pl.multiple_of(dst, SUBLANES), SUBLANES)], sem_row.at[sl])

    def drain(sl):
        def wait_chunk(c, carry):
            chunk_copy(sl * slot_rows, 0, sl).wait()
            return carry

        def wait_row(c, carry):
            row_copy(sl * slot_rows, 0, sl).wait()
            return carry

        lax.fori_loop(0, npend_smem[2 * sl], wait_chunk, 0)
        lax.fori_loop(0, npend_smem[2 * sl + 1], wait_row, 0)

    @pl.when(i >= 2)
    def _():
        drain(slot)

    def fill_delta(ex, off):
        delta_smem[ex] = off - tbase_ref[i * N_EXPERTS + ex]
        return off + rcnt_ref[i * N_EXPERTS + ex]

    lax.fori_loop(0, N_EXPERTS, fill_delta, 0)
    e = eidx_ref[...]
    delta = jnp.zeros_like(e)
    for ex in range(N_EXPERTS):
        delta = jnp.where(e == ex, delta_smem[ex], delta)
    li_vmem[...] = (rank_ref[...] + delta) * SUBLANES + slot * slot_rows
    to_smem = pltpu.make_async_copy(li_vmem, li_smem, sem_idx)
    to_smem.start()
    to_smem.wait()

    for tok in range(tm):
        row = xn3_ref[tok * SUBLANES:(tok + 1) * SUBLANES, :]
        for k in range(TOP_K):
            stage[pl.ds(pl.multiple_of(li_smem[k, tok], SUBLANES), SUBLANES), :] = row

    def send_expert(ex, carry):
        src, n_chunks, n_rows = carry
        c = rcnt_ref[i * N_EXPERTS + ex]
        dst = rstart_ref[i * N_EXPERTS + ex] * SUBLANES
        n_full = c >> RUN_SHIFT
        n_rem = c & (RUN_CHUNK - 1)

        def one_chunk(ch, cc):
            chunk_copy(src + ch * chunk_rows, dst + ch * chunk_rows, slot).start()
            return cc

        def one_row(r, cc):
            o = n_full * chunk_rows + r * SUBLANES
            row_copy(src + o, dst + o, slot).start()
            return cc

        lax.fori_loop(0, n_full, one_chunk, 0)
        lax.fori_loop(0, n_rem, one_row, 0)
        return src + c * SUBLANES, n_chunks + n_full, n_rows + n_rem

    _, n_chunks, n_rows = lax.fori_loop(0, N_EXPERTS, send_expert, (slot * slot_rows, 0, 0))
    npend_smem[2 * slot] = n_chunks
    npend_smem[2 * slot + 1] = n_rows

    @pl.when(i == n_steps - 1)
    def _():
        drain(slot)
        if n_steps >= 2:
            drain(1 - slot)


def _dispatch(pstart, cnt, rstart, rcnt, tbase, eidx, ranks, xn3, n_rows, tm=ROUTE_TILE):
    t = eidx.shape[1]
    meta = pl.BlockSpec((SUBLANES, tm), lambda i, *_: (0, i))
    grid_spec = pltpu.PrefetchScalarGridSpec(
        num_scalar_prefetch=5,
        grid=(t // tm,),
        in_specs=[meta, meta, pl.BlockSpec((tm * SUBLANES, LANES), lambda i, *_: (i, 0))],
        out_specs=pl.BlockSpec(memory_space=pl.ANY),
        scratch_shapes=[pltpu.VMEM((SUBLANES, tm), I32), pltpu.SMEM((SUBLANES, tm), I32),
                        pltpu.SMEM((N_EXPERTS,), I32), pltpu.SMEM((4,), I32),
                        pltpu.VMEM((2 * TOP_K * tm * SUBLANES, LANES), F32),
                        pltpu.VMEM((ROW_BLOCK * SUBLANES, LANES), F32),
                        pltpu.SemaphoreType.DMA, pltpu.SemaphoreType.DMA((2,)), pltpu.SemaphoreType.DMA((2,)),
                        pltpu.SemaphoreType.DMA, pltpu.SemaphoreType.DMA],
    )
    return pl.pallas_call(
        functools.partial(_dispatch_kernel, n_steps=t // tm),
        grid_spec=grid_spec,
        out_shape=jax.ShapeDtypeStruct((n_rows * SUBLANES, LANES), F32),
        compiler_params=_cparams(("arbitrary",)),
        name="dispatch",
    )(pstart, cnt, rstart, rcnt, tbase, eidx, ranks, xn3)


def _experts_kernel(bexp_ref, nblk_ref, nbe_ref, xs_ref, bup_ref, bdn_ref, wup_hbm, wdn_hbm, y_ref,
                    wup32, wdn32, wup16, wdn16, x16, sem_up, sem_dn):
    i = pl.program_id(0)
    rows = x16.shape[0]
    d = x16.shape[1]
    de = wdn16.shape[0]
    e = bexp_ref[i]
    active = i < nblk_ref[0]
    first = (i == 0) | (e != bexp_ref[jnp.maximum(i - 1, 0)])

    def fetch(ex):
        return (pltpu.make_async_copy(wup_hbm.at[ex], wup32, sem_up),
                pltpu.make_async_copy(wdn_hbm.at[ex], wdn32, sem_dn))

    @pl.when(i == 0)
    def _():
        for cp in fetch(e):
            cp.start()

    @pl.when(active & first)
    def _():
        for cp in fetch(e):
            cp.wait()
        wup16[...] = wup32[...].astype(BF16)
        wdn16[...] = wdn32[...].astype(BF16)
        nxt = i + nbe_ref[e]

        @pl.when(nxt < nblk_ref[0])
        def _():
            for cp in fetch(bexp_ref[nxt]):
                cp.start()

    @pl.when(active)
    def _():
        for j in range(d // LANES):
            x16[:, j * LANES:(j + 1) * LANES] = xs_ref[pl.ds(j, rows, stride=SUBLANES), :].astype(BF16)
        hdn = jnp.dot(x16[...], wup16[...], preferred_element_type=F32) + bup_ref[0]
        g = jnp.minimum(hdn[:, :de], SWIGLU_LIMIT)
        u = jnp.clip(hdn[:, de:], -SWIGLU_LIMIT, SWIGLU_LIMIT)
        glu = g * jax.nn.sigmoid(SWIGLU_ALPHA * g)
        act = ((u + 1.0) * glu).astype(BF16)
        y = jnp.dot(act, wdn16[...], preferred_element_type=F32) + bdn_ref[0]
        for j in range(d // LANES):
            y_ref[pl.ds(j, rows, stride=SUBLANES), :] = y[:, j * LANES:(j + 1) * LANES]

    @pl.when(jnp.logical_not(active))
    def _():
        y_ref[...] = jnp.zeros_like(y_ref)


def _experts(bexp, nblk, nbe, xs, w_up, b_up, w_down, b_down):
    n_exp, d, d2 = w_up.shape
    de = w_down.shape[1]
    n_blocks = bexp.shape[0]
    rows = ROW_BLOCK
    xspec = pl.BlockSpec((rows * SUBLANES, LANES), lambda i, be, nb, ne: (i, 0))
    any_spec = pl.BlockSpec(memory_space=pl.ANY)
    grid_spec = pltpu.PrefetchScalarGridSpec(
        num_scalar_prefetch=3,
        grid=(n_blocks,),
        in_specs=[xspec,
                  pl.BlockSpec((1, 1, d2), lambda i, be, nb, ne: (be[i], 0, 0)),
                  pl.BlockSpec((1, 1, d), lambda i, be, nb, ne: (be[i], 0, 0)),
                  any_spec, any_spec],
        out_specs=xspec,
        scratch_shapes=[pltpu.VMEM((d, d2), F32), pltpu.VMEM((de, d), F32),
                        pltpu.VMEM((d, d2), BF16), pltpu.VMEM((de, d), BF16), pltpu.VMEM((rows, d), BF16),
                        pltpu.SemaphoreType.DMA, pltpu.SemaphoreType.DMA],
    )
    return pl.pallas_call(
        _experts_kernel,
        grid_spec=grid_spec,
        out_shape=jax.ShapeDtypeStruct(xs.shape, F32),
        compiler_params=_cparams(("arbitrary",)),
        name="experts",
    )(bexp, nblk, nbe, xs, b_up[:, None, :], b_down[:, None, :], w_up, w_down)


def _combine_kernel(rstart_ref, rcnt_ref, tbase_ref, eidx_ref, rank_ref, gate_ref, h_ref, p_ref, pg_ref, wpg_ref,
                    wp_ref, fg_ref, y_hbm, out_ref,
                    stage, li_vmem, li_smem, gate_smem, delta_smem, nchunk_smem, moe_rows, sem_idx, sem_runs):
    tm = h_ref.shape[0]
    d = h_ref.shape[1]
    i = pl.program_id(0)
    n = pl.num_programs(0)
    slot = lax.rem(i, 2)
    chunk_rows = RUN_CHUNK * SUBLANES
    slot_rows = stage.shape[0] // 2

    def chunks_of(c):
        return (c + (RUN_CHUNK - 1)) >> RUN_SHIFT

    def chunk_copy(src, dst, sl):
        return pltpu.make_async_copy(y_hbm.at[pl.ds(pl.multiple_of(src, SUBLANES), chunk_rows)],
                                     stage.at[pl.ds(pl.multiple_of(dst, SUBLANES), chunk_rows)], sem_runs.at[sl])

    def start_runs(tile, sl):
        def per_expert(ex, dst):
            src = rstart_ref[tile * N_EXPERTS + ex] * SUBLANES
            n_ch = chunks_of(rcnt_ref[tile * N_EXPERTS + ex])

            def one(ch, carry):
                chunk_copy(src + ch * chunk_rows, dst + ch * chunk_rows, sl).start()
                return carry

            lax.fori_loop(0, n_ch, one, 0)
            return dst + n_ch * chunk_rows

        end = lax.fori_loop(0, N_EXPERTS, per_expert, sl * slot_rows)
        nchunk_smem[sl] = (end - sl * slot_rows) >> (RUN_SHIFT + 3)

    @pl.when(i == 0)
    def _():
        start_runs(0, 0)

    @pl.when(i + 1 < n)
    def _():
        start_runs(i + 1, 1 - slot)

    def fill_delta(ex, off):
        delta_smem[ex] = off - tbase_ref[i * N_EXPERTS + ex]
        return off + chunks_of(rcnt_ref[i * N_EXPERTS + ex]) * RUN_CHUNK

    lax.fori_loop(0, N_EXPERTS, fill_delta, 0)
    e = eidx_ref[...]
    delta = jnp.zeros_like(e)
    for ex in range(N_EXPERTS):
        delta = jnp.where(e == ex, delta_smem[ex], delta)
    li_vmem[...] = (rank_ref[...] + delta) * SUBLANES + slot * slot_rows
    to_smem = [pltpu.make_async_copy(li_vmem, li_smem, sem_idx.at[0]),
               pltpu.make_async_copy(gate_ref, gate_smem, sem_idx.at[1])]
    for cp in to_smem:
        cp.start()
    for cp in to_smem:
        cp.wait()

    def wait_chunk(ch, carry):
        chunk_copy(0, slot * slot_rows, slot).wait()
        return carry

    lax.fori_loop(0, nchunk_smem[slot], wait_chunk, 0)

    for tok in range(tm):
        acc = None
        for k in range(TOP_K):
            row = stage[pl.ds(pl.multiple_of(li_smem[k, tok], SUBLANES), SUBLANES), :]
            term = gate_smem[k, tok] * row
            acc = term if acc is None else acc + term
        moe_rows[tok * SUBLANES:(tok + 1) * SUBLANES, :] = acc

    moe = jnp.concatenate([moe_rows[pl.ds(j, tm, stride=SUBLANES), :] for j in range(d // LANES)], axis=1)
    h = h_ref[...] + moe

    ple = jnp.dot(p_ref[...].astype(BF16), wp_ref[...], preferred_element_type=F32)
    ple_gate = jax.nn.sigmoid(jnp.dot(_rms(h, pg_ref[...]).astype(BF16), wpg_ref[...],
                                      preferred_element_type=F32))
    h = h + ple * ple_gate
    out_ref[...] = _rms(h, fg_ref[...])


def _combine(rstart, rcnt, tbase, eidx, ranks, gates, h1, p2, pg, wpg, wp, fg, y, tm=ROUTE_TILE):
    t, d = h1.shape
    dp = p2.shape[1]
    row = lambda i, *_: (i, 0)
    col = lambda i, *_: (0, i)
    const = lambda i, *_: (0, 0)
    meta = pl.BlockSpec((SUBLANES, tm), col)
    assert RUN_CHUNK <= N_EXPERTS
    stage_rows = (TOP_K * tm + N_EXPERTS * RUN_CHUNK) * SUBLANES
    grid_spec = pltpu.PrefetchScalarGridSpec(
        num_scalar_prefetch=3,
        grid=(t // tm,),
        in_specs=[meta, meta, meta, pl.BlockSpec((tm, d), row), pl.BlockSpec((tm, dp), row),
                  pl.BlockSpec((1, d), const), pl.BlockSpec((d, d), const), pl.BlockSpec((dp, d), const),
                  pl.BlockSpec((1, d), const), pl.BlockSpec(memory_space=pl.ANY)],
        out_specs=pl.BlockSpec((tm, d), row),
        scratch_shapes=[pltpu.VMEM((2 * stage_rows, LANES), F32), pltpu.VMEM((SUBLANES, tm), I32),
                        pltpu.SMEM((SUBLANES, tm), I32), pltpu.SMEM((SUBLANES, tm), F32),
                        pltpu.SMEM((N_EXPERTS,), I32), pltpu.SMEM((2,), I32),
                        pltpu.VMEM((tm * SUBLANES, LANES), F32),
                        pltpu.SemaphoreType.DMA((2,)), pltpu.SemaphoreType.DMA((2,))],
    )
    return pl.pallas_call(
        _combine_kernel,
        grid_spec=grid_spec,
        out_shape=jax.ShapeDtypeStruct((t, d), F32),
        compiler_params=_cparams(("arbitrary",)),
        name="combine",
    )(rstart, rcnt, tbase, eidx, ranks, gates, h1, p2, pg, wpg, wp, fg, y)


def _block_diag(w):
    n, d, _ = w.shape
    eye = jnp.eye(n, dtype=w.dtype)
    return (eye[:, None, :, None] * w[:, :, None, :]).reshape(n * d, n * d)


def _stages(x, p, mix_norm_g, w_in, conv_w, conv_b, lru_w_a, lru_b_a, lru_w_x, lru_b_x, lru_lambda,
            lru_out_g, sb_out_g, w_out, ffn_norm_g, w_router, b_router, w_up, b_up, w_down, b_down,
            ple_norm_g, w_ple_gate, w_ple, final_norm_g):
    b, s, d = x.shape
    t = b * s
    st = {}
    x2 = x.reshape(t, d)
    lx, lg, q, k, v = _in_proj(x2, mix_norm_g[0][None], w_in[0].astype(BF16))
    st.update(lru_x=lx, lru_gate=lg, q=q, k=k, v=v)
    row = lambda a: a[None].astype(F32)
    lru_n = _lru(lx.reshape(b, s, D_LRU), lg.reshape(b, s, D_LRU), conv_w[0], row(conv_b[0]),
                 _block_diag(lru_w_a[0]).astype(BF16), row(lru_b_a[0]),
                 _block_diag(lru_w_x[0]).astype(BF16), row(lru_b_x[0]),
                 row(lru_lambda[0]), row(lru_out_g[0]))
    st["lru_n"] = lru_n
    sb_y = _sb_attn(q.reshape(b, s, D_SB), k.reshape(b, s, D_SB), v.reshape(b, s, D_SB))
    st["sb_y"] = sb_y
    wo = w_out[0].astype(BF16)
    h1, xn3, eidx, gates, ranks, counts, tile_base = _out_route(
        x2, lru_n.reshape(t, D_LRU), sb_y.reshape(t, D_SB), row(sb_out_g[0]), wo[:D_LRU], wo[D_LRU:],
        row(ffn_norm_g[0]), w_router[0].T, b_router[0][:, None])
    st.update(h1=h1, xn1=xn3, eidx=eidx, gates=gates, ranks=ranks, counts=counts)

    cnt = counts[:, 0].astype(I32)
    padded = (cnt + ROW_BLOCK - 1) // ROW_BLOCK * ROW_BLOCK
    pend = jnp.cumsum(padded)
    pstart = pend - padded
    n_blocks = -(-(t * TOP_K) // ROW_BLOCK) + N_EXPERTS
    block_row0 = jnp.arange(n_blocks, dtype=I32) * ROW_BLOCK
    bexp = jnp.minimum(jnp.sum((pend[None, :] <= block_row0[:, None]).astype(I32), axis=1), N_EXPERTS - 1)
    nblk = (pend[-1:] // ROW_BLOCK).astype(I32)

    tbase = tile_base[:, 0].astype(I32).reshape(-1, N_EXPERTS)
    rcnt = jnp.concatenate([tbase[1:], cnt[None, :]], axis=0) - tbase
    rstart = pstart[None, :] + tbase

    xs = _dispatch(pstart, cnt, rstart.reshape(-1), rcnt.reshape(-1), tbase.reshape(-1), eidx, ranks, xn3,
                   n_blocks * ROW_BLOCK)
    y = _experts(bexp, nblk, padded // ROW_BLOCK, xs, w_up[0], b_up[0], w_down[0], b_down[0])
    st.update(xs=xs, y=y)
    out = _combine(rstart.reshape(-1), rcnt.reshape(-1), tbase.reshape(-1), eidx, ranks, gates, h1,
                   p[0].reshape(t, -1), row(ple_norm_g[0]), w_ple_gate[0].astype(BF16),
                   w_ple[0].astype(BF16), row(final_norm_g), y)
    st["final"] = out.reshape(b, s, d)
    return st


def kernel(x, p, mix_norm_g, w_in, conv_w, conv_b, lru_w_a, lru_b_a, lru_w_x, lru_b_x, lru_lambda,
           lru_out_g, sb_out_g, w_out, ffn_norm_g, w_router, b_router, w_up, b_up, w_down, b_down,
           ple_norm_g, w_ple_gate, w_ple, final_norm_g):
    return _stages(x, p, mix_norm_g, w_in, conv_w, conv_b, lru_w_a, lru_b_a, lru_w_x, lru_b_x, lru_lambda,
                   lru_out_g, sb_out_g, w_out, ffn_norm_g, w_router, b_router, w_up, b_up, w_down, b_down,
                   ple_norm_g, w_ple_gate, w_ple, final_norm_g)["final"]
```

```python
import functools

import jax
import jax.numpy as jnp
from jax import lax
from jax.experimental import pallas as pl
from jax.experimental.pallas import tpu as pltpu

F32 = jnp.float32
BF16 = jnp.bfloat16
I32 = jnp.int32

RMS_EPS = 1e-6
LANES = 128
SUBLANES = 8
D_LRU = 512
D_SB = 512
N_HEADS = 8
HEAD_DIM = 64
CONV_WIDTH = 4
LRU_C = 8.0
N_EXPERTS = 32
TOP_K = 4
SWIGLU_LIMIT = 7.0
SWIGLU_ALPHA = 1.702
ROW_BLOCK = 256
ROUTE_TILE = 512
RUN_SHIFT = 5
RUN_CHUNK = 1 << RUN_SHIFT
SEND_SHIFT = 4
SEND_CHUNK = 1 << SEND_SHIFT
VMEM_LIMIT = 56 * 1024 * 1024


def _rms(x, g):
    return (x * lax.rsqrt(jnp.mean(x * x, axis=-1, keepdims=True) + RMS_EPS)) * g


def _cparams(sem, flags=None):
    return pltpu.CompilerParams(dimension_semantics=sem, vmem_limit_bytes=VMEM_LIMIT, flags=flags)


def _in_proj_kernel(x_ref, g_ref, w_ref, lx_ref, lg_ref, q_ref, k_ref, v_ref):
    xn = _rms(x_ref[...], g_ref[...])
    proj = jnp.dot(xn.astype(BF16), w_ref[...], preferred_element_type=F32)
    lx_ref[...] = proj[:, 0:D_LRU]
    lg_ref[...] = proj[:, D_LRU:2 * D_LRU]
    o = 2 * D_LRU
    q_ref[...] = proj[:, o:o + D_SB].astype(BF16)
    k_ref[...] = proj[:, o + D_SB:o + 2 * D_SB].astype(BF16)
    v_ref[...] = proj[:, o + 2 * D_SB:o + 3 * D_SB].astype(BF16)


def _in_proj(x2, g, w_bf, tm=512):
    t, d = x2.shape
    n = w_bf.shape[1]
    row = lambda i: (i, 0)
    const = lambda i: (0, 0)
    return pl.pallas_call(
        _in_proj_kernel,
        grid=(t // tm,),
        in_specs=[pl.BlockSpec((tm, d), row), pl.BlockSpec((1, d), const), pl.BlockSpec((d, n), const)],
        out_specs=[pl.BlockSpec((tm, D_LRU), row)] * 2 + [pl.BlockSpec((tm, D_SB), row)] * 3,
        out_shape=[jax.ShapeDtypeStruct((t, D_LRU), F32)] * 2 + [jax.ShapeDtypeStruct((t, D_SB), BF16)] * 3,
        compiler_params=_cparams(("parallel",)),
        name="in_proj",
    )(x2, g, w_bf)


def _shift_rows(x, k, fill):
    if k % SUBLANES == 0:
        return jnp.concatenate([jnp.full((k, x.shape[1]), fill, x.dtype), x[:x.shape[0] - k]], axis=0)
    rolled = pltpu.roll(x, k, 0)
    rows = lax.broadcasted_iota(I32, x.shape, 0)
    return jnp.where(rows >= k, rolled, fill)


def _lru_kernel(lx_ref, lg_ref, cw_ref, cb_ref, wa_ref, ba_ref, wx_ref, bx_ref, lam_ref, og_ref,
                out_ref, tail_ref, h_ref):
    ts = lx_ref.shape[1]

    @pl.when(pl.program_id(1) == 0)
    def _():
        tail_ref[...] = jnp.zeros_like(tail_ref)
        h_ref[...] = jnp.zeros_like(h_ref)

    x = lx_ref[0]
    tail = tail_ref[...]
    rows = lax.broadcasted_iota(I32, x.shape, 0)
    cw = cw_ref[...]
    conv = x * cw[CONV_WIDTH - 1:CONV_WIDTH, :] + cb_ref[...]
    for k in range(1, CONV_WIDTH):
        cur = pltpu.roll(x, k, 0)
        prev = pltpu.roll(tail, k, 0)
        prev_full = jnp.concatenate([prev] + [prev] * (ts // SUBLANES - 1), axis=0)
        shifted = jnp.where(rows >= k, cur, prev_full)
        conv = conv + shifted * cw[CONV_WIDTH - 1 - k:CONV_WIDTH - k, :]
    tail_ref[...] = x[ts - SUBLANES:, :]

    cb16 = conv.astype(BF16)
    r = jax.nn.sigmoid(jnp.dot(cb16, wa_ref[...], preferred_element_type=F32) + ba_ref[...])
    gi = jax.nn.sigmoid(jnp.dot(cb16, wx_ref[...], preferred_element_type=F32) + bx_ref[...])
    lam = lam_ref[...]
    softplus_neg = jnp.maximum(-lam, 0.0) + jnp.log1p(jnp.exp(-jnp.abs(lam)))
    log_a = (-LRU_C * r) * softplus_neg
    a = jnp.exp(log_a)
    b = jnp.sqrt(1.0 - jnp.exp(2.0 * log_a)) * (gi * conv)

    k = 1
    while k < ts:
        a_sh = _shift_rows(a, k, 1.0)
        b_sh = _shift_rows(b, k, 0.0)
        b = a * b_sh + b
        a = a * a_sh
        k *= 2
    h = a * h_ref[0:1, :] + b
    h_ref[...] = jnp.broadcast_to(h[ts - 1:ts, :], h_ref.shape)

    gate = lg_ref[0]
    y = h * jax.nn.gelu(gate)
    out_ref[0] = _rms(y, og_ref[...]).astype(out_ref.dtype)


def _lru(lx, lg, cw, cb, wa, ba, wx, bx, lam, og, ts=256):
    b, s, c = lx.shape
    tile = lambda i, j: (i, j, 0)
    const = lambda i, j: (0, 0)
    vec = pl.BlockSpec((1, c), const)
    return pl.pallas_call(
        _lru_kernel,
        grid=(b, s // ts),
        in_specs=[pl.BlockSpec((1, ts, c), tile), pl.BlockSpec((1, ts, c), tile),
                  pl.BlockSpec((CONV_WIDTH, c), const), vec,
                  pl.BlockSpec((c, c), const), vec, pl.BlockSpec((c, c), const), vec, vec, vec],
        out_specs=pl.BlockSpec((1, ts, c), tile),
        out_shape=jax.ShapeDtypeStruct((b, s, c), BF16),
        scratch_shapes=[pltpu.VMEM((SUBLANES, c), F32), pltpu.VMEM((SUBLANES, c), F32)],
        compiler_params=_cparams(("parallel", "arbitrary")),
        name="lru",
    )(lx, lg, cw, cb, wa, ba, wx, bx, lam, og)


def _split_bf16(x):
    hi = x.astype(BF16)
    lo = (x - hi.astype(F32)).astype(BF16)
    return hi, lo


def _sb_attn_kernel(q_ref, k_ref, v_ref, o_ref, acc_ref, run_ref, *, tq, tk, slabs):
    qi = pl.program_id(2)
    lane = lax.broadcasted_iota(I32, (1, LANES), 1)
    head_masks = [lane < HEAD_DIM, lane >= HEAD_DIM]
    zero = jnp.zeros((), BF16)
    qs = []
    for sl in range(slabs):
        q = q_ref[0, :, sl * LANES:(sl + 1) * LANES] * jnp.asarray(HEAD_DIM ** -0.5, BF16)
        qs.append([jnp.where(m, q, zero) for m in head_masks])

    r = lax.broadcasted_iota(I32, (tk, tk), 0)
    c = lax.broadcasted_iota(I32, (tk, tk), 1)
    suffix = jnp.where(r > c, 1.0, 0.0).astype(BF16)

    acc_ref[...] = jnp.zeros_like(acc_ref)
    run_ref[...] = jnp.zeros_like(run_ref)

    def tile(j, diagonal):
        rows = pl.ds(pl.multiple_of(j * tk, tk), tk)
        if diagonal:
            qpos = lax.broadcasted_iota(I32, (tq, tk), 0)
            kpos = lax.broadcasted_iota(I32, (tq, tk), 1)
            visible = kpos < qpos
        for sl in range(slabs):
            ks = k_ref[0, rows, sl * LANES:(sl + 1) * LANES]
            vs = v_ref[0, rows, sl * LANES:(sl + 1) * LANES]
            pv = None
            for hd in range(2):
                z = lax.dot_general(qs[sl][hd], ks, (((1,), (1,)), ((), ())), preferred_element_type=F32)
                drop = jnp.maximum(z, 0.0) + jnp.log(1.0 + jnp.exp(-jnp.abs(z)))
                if diagonal:
                    drop = jnp.where(visible, drop, 0.0)
                sums = jnp.dot(drop.astype(BF16), suffix, preferred_element_type=F32)
                run = run_ref[2 * sl + hd]
                w = jnp.exp(z - ((drop + sums) + run))
                if diagonal:
                    w = jnp.where(visible, w, 0.0)
                vh = jnp.where(head_masks[hd], vs, zero)
                part = jnp.dot(w.astype(BF16), vh, preferred_element_type=F32)
                pv = part if pv is None else pv + part
                run_ref[2 * sl + hd] = run + (sums[:, 0:1] + drop[:, 0:1])
            acc_ref[:, sl * LANES:(sl + 1) * LANES] += pv

    tile(qi, True)

    def body(i, carry):
        tile(qi - 1 - i, False)
        return carry

    lax.fori_loop(0, qi, body, 0)
    o_ref[0] = acc_ref[...]


def _sb_attn(q, k, v, tq=512, slabs=4):
    b, s, c = q.shape
    tk = tq
    w = LANES * slabs
    qspec = pl.BlockSpec((1, tq, w), lambda i, j, l: (i, l, j))
    kvspec = pl.BlockSpec((1, s, w), lambda i, j, l: (i, 0, j))
    return pl.pallas_call(
        functools.partial(_sb_attn_kernel, tq=tq, tk=tk, slabs=slabs),
        grid=(b, c // w, s // tq),
        in_specs=[qspec, kvspec, kvspec],
        out_specs=qspec,
        out_shape=jax.ShapeDtypeStruct((b, s, c), F32),
        scratch_shapes=[pltpu.VMEM((tq, w), F32), pltpu.VMEM((2 * slabs, tq, 1), F32)],
        compiler_params=_cparams(("parallel", "parallel", "arbitrary")),
        name="sb_attn",
    )(q, k, v)


def _out_route_kernel(x_ref, lru_ref, sb_ref, sbg_ref, wol_ref, wos_ref, fg_ref, wr_ref, br_ref,
                      h_ref, xn3_ref, eidx_ref, gate_ref, rank_ref, cnt_ref, tbase_ref, cnt_scr):
    tm = x_ref.shape[0]

    @pl.when(pl.program_id(0) == 0)
    def _():
        cnt_scr[...] = jnp.zeros_like(cnt_scr)

    sbn = _rms(sb_ref[...], sbg_ref[...]).astype(BF16)
    h = (x_ref[...] + jnp.dot(lru_ref[...], wol_ref[...], preferred_element_type=F32)
         + jnp.dot(sbn, wos_ref[...], preferred_element_type=F32))
    h_ref[...] = h
    xn = _rms(h, fg_ref[...])
    for j in range(xn.shape[1] // LANES):
        xn3_ref[pl.ds(j, tm, stride=SUBLANES), :] = xn[:, j * LANES:(j + 1) * LANES]

    nt = (((1,), (1,)), ((), ()))
    w_hi, w_lo = _split_bf16(wr_ref[...])
    x_hi, x_lo = _split_bf16(xn)
    logits = (lax.dot_general(w_hi, x_hi, nt, preferred_element_type=F32)
              + (lax.dot_general(w_hi, x_lo, nt, preferred_element_type=F32)
                 + lax.dot_general(w_lo, x_hi, nt, preferred_element_type=F32))) + br_ref[...]
    n_exp = logits.shape[0]
    eio = lax.broadcasted_iota(I32, logits.shape, 0)
    work = logits
    vals, hits, idxs = [], [], []
    for _k in range(TOP_K):
        m = jnp.max(work, axis=0, keepdims=True)
        idx = jnp.min(jnp.where(work == m, eio, n_exp), axis=0, keepdims=True)
        hit = eio == idx
        work = jnp.where(hit, -jnp.inf, work)
        vals.append(m)
        hits.append(hit)
        idxs.append(idx)
    exps = [jnp.exp(v - vals[0]) for v in vals]
    denom = exps[0] + exps[1] + exps[2] + exps[3]
    gates = [e / denom for e in exps]

    onehot = jnp.where(hits[0] | hits[1] | hits[2] | hits[3], 1.0, 0.0).astype(BF16)
    r = lax.broadcasted_iota(I32, (tm, tm + LANES), 0)
    c = lax.broadcasted_iota(I32, (tm, tm + LANES), 1)
    prefix_mat = jnp.where((r < c) | (c >= tm), 1.0, 0.0).astype(BF16)
    sums = jnp.dot(onehot, prefix_mat, preferred_element_type=F32)
    base = cnt_scr[...]
    pos = sums[:, :tm] + jnp.concatenate([base] * (tm // LANES), axis=1)
    ranks = [jnp.sum(jnp.where(hk, pos, 0.0), axis=0, keepdims=True) for hk in hits]
    tbase_ref[...] = base
    cnt_scr[...] = base + sums[:, tm:]
    cnt_ref[...] = cnt_scr[...]

    pad_i = jnp.zeros((SUBLANES - TOP_K, tm), I32)
    eidx_ref[...] = jnp.concatenate(idxs + [pad_i], axis=0)
    rank_ref[...] = jnp.concatenate([rk.astype(I32) for rk in ranks] + [pad_i], axis=0)
    gate_ref[...] = jnp.concatenate(gates + [jnp.zeros((SUBLANES - TOP_K, tm), F32)], axis=0)


def _out_route(x2, lru_n, sb_y, sbg, wol, wos, fg, wr_t, br, tm=ROUTE_TILE):
    t, d = x2.shape
    e = wr_t.shape[0]
    row = lambda i: (i, 0)
    col = lambda i: (0, i)
    const = lambda i: (0, 0)
    meta = pl.BlockSpec((SUBLANES, tm), col)
    return pl.pallas_call(
        _out_route_kernel,
        grid=(t // tm,),
        in_specs=[pl.BlockSpec((tm, d), row), pl.BlockSpec((tm, D_LRU), row), pl.BlockSpec((tm, D_SB), row),
                  pl.BlockSpec((1, D_SB), const), pl.BlockSpec((D_LRU, d), const), pl.BlockSpec((D_SB, d), const),
                  pl.BlockSpec((1, d), const), pl.BlockSpec((e, d), const), pl.BlockSpec((e, 1), const)],
        out_specs=[pl.BlockSpec((tm, d), row), pl.BlockSpec((tm * SUBLANES, LANES), row), meta, meta, meta,
                   pl.BlockSpec((e, LANES), const), pl.BlockSpec((e, LANES), row)],
        out_shape=[jax.ShapeDtypeStruct((t, d), F32), jax.ShapeDtypeStruct((t * SUBLANES, LANES), F32),
                   jax.ShapeDtypeStruct((SUBLANES, t), I32), jax.ShapeDtypeStruct((SUBLANES, t), F32),
                   jax.ShapeDtypeStruct((SUBLANES, t), I32), jax.ShapeDtypeStruct((e, LANES), F32),
                   jax.ShapeDtypeStruct((t // tm * e, LANES), F32)],
        scratch_shapes=[pltpu.VMEM((e, LANES), F32)],
        compiler_params=_cparams(("arbitrary",)),
        name="out_route",
    )(x2, lru_n, sb_y, sbg, wol, wos, fg, wr_t, br)


def _row_tile(ref, row):
    return ref.at[pl.ds(pl.multiple_of(row * SUBLANES, SUBLANES), SUBLANES)]


def _dispatch_kernel(pstart_ref, cnt_ref, rstart_ref, rcnt_ref, tbase_ref, eidx_ref, rank_ref, xn3_ref, xs_out,
                     li_vmem, li_smem, delta_smem, npend_smem, stage, zero_tile,
                     sem_idx, sem_chunk, sem_row, sem_pad, sem_blk, *, n_steps):
    tm = eidx_ref.shape[1]

    @pl.when(pl.program_id(0) == 0)
    def _():
        zero_tile[...] = jnp.zeros_like(zero_tile)

        def pad_copy(row):
            return pltpu.make_async_copy(zero_tile.at[pl.ds(0, SUBLANES)], _row_tile(xs_out, row), sem_pad)

        def pads_of(ex, action):
            first_pad = pstart_ref[ex] + cnt_ref[ex]
            n_pad = (0 - cnt_ref[ex]) & (ROW_BLOCK - 1)

            def one(r, c):
                action(pad_copy(first_pad + r))
                return c

            lax.fori_loop(0, n_pad, one, 0)

        def fill_expert(ex, carry):
            pads_of(ex, lambda cp: cp.start())
            return carry

        def drain_expert(ex, carry):
            pads_of(ex, lambda cp: cp.wait())
            return carry

        lax.fori_loop(0, N_EXPERTS, fill_expert, 0)

        last = N_EXPERTS - 1
        used_rows = pstart_ref[last] + cnt_ref[last] + ((0 - cnt_ref[last]) & (ROW_BLOCK - 1))
        block_rows = ROW_BLOCK * SUBLANES

        def block_copy(blk):
            return pltpu.make_async_copy(
                zero_tile, xs_out.at[pl.ds(pl.multiple_of(blk * block_rows, block_rows), block_rows)], sem_blk)

        def fill_block(blk, c):
            block_copy(blk).start()
            return c

        def drain_block(blk, c):
            block_copy(blk).wait()
            return c

        first_free = used_rows // ROW_BLOCK
        lax.fori_loop(first_free, xs_out.shape[0] // block_rows, fill_block, 0)
        lax.fori_loop(0, N_EXPERTS, drain_expert, 0)
        lax.fori_loop(first_free, xs_out.shape[0] // block_rows, drain_block, 0)

    i = pl.program_id(0)
    slot = lax.rem(i, 2)
    chunk_rows = SEND_CHUNK * SUBLANES
    slot_rows = stage.shape[0] // 2

    def chunk_copy(src, dst, sl):
        return pltpu.make_async_copy(stage.at[pl.ds(pl.multiple_of(src, SUBLANES), chunk_rows)],
                                     xs_out.at[pl.ds(pl.multiple_of(dst, SUBLANES), chunk_rows)], sem_chunk.at[sl])

    def row_copy(src, dst, sl):
        return pltpu.make_async_copy(stage.at[pl.ds(pl.multiple_of(src, SUBLANES), SUBLANES)],
                                     xs_out.at[pl.ds(pl.multiple_of(dst, SUBLANES), SUBLANES)], sem_row.at[sl])

    def drain(sl):
        def wait_chunk(c, carry):
            chunk_copy(sl * slot_rows, 0, sl).wait()
            return carry

        def wait_row(c, carry):
            row_copy(sl * slot_rows, 0, sl).wait()
            return carry

        lax.fori_loop(0, npend_smem[2 * sl], wait_chunk, 0)
        lax.fori_loop(0, npend_smem[2 * sl + 1], wait_row, 0)

    @pl.when(i >= 2)
    def _():
        drain(slot)

    def fill_delta(ex, off):
        delta_smem[ex] = off - tbase_ref[i * N_EXPERTS + ex]
        return off + rcnt_ref[i * N_EXPERTS + ex]

    lax.fori_loop(0, N_EXPERTS, fill_delta, 0)
    e = eidx_ref[...]
    delta = jnp.zeros_like(e)
    for ex in range(N_EXPERTS):
        delta = jnp.where(e == ex, delta_smem[ex], delta)
    li_vmem[...] = (rank_ref[...] + delta) * SUBLANES + slot * slot_rows
    to_smem = pltpu.make_async_copy(li_vmem, li_smem, sem_idx)
    to_smem.start()
    to_smem.wait()

    for tok in range(tm):
        row = xn3_ref[tok * SUBLANES:(tok + 1) * SUBLANES, :]
        for k in range(TOP_K):
            stage[pl.ds(pl.multiple_of(li_smem[k, tok], SUBLANES), SUBLANES), :] = row

    def send_expert(ex, carry):
        src, n_chunks, n_rows = carry
        c = rcnt_ref[i * N_EXPERTS + ex]
        dst = rstart_ref[i * N_EXPERTS + ex] * SUBLANES
        n_full = c >> SEND_SHIFT
        n_rem = c & (SEND_CHUNK - 1)

        def one_chunk(ch, cc):
            chunk_copy(src + ch * chunk_rows, dst + ch * chunk_rows, slot).start()
            return cc

        def one_row(r, cc):
            o = n_full * chunk_rows + r * SUBLANES
            row_copy(src + o, dst + o, slot).start()
            return cc

        lax.fori_loop(0, n_full, one_chunk, 0)
        lax.fori_loop(0, n_rem, one_row, 0)
        return src + c * SUBLANES, n_chunks + n_full, n_rows + n_rem

    _, n_chunks, n_rows = lax.fori_loop(0, N_EXPERTS, send_expert, (slot * slot_rows, 0, 0))
    npend_smem[2 * slot] = n_chunks
    npend_smem[2 * slot + 1] = n_rows

    @pl.when(i == n_steps - 1)
    def _():
        drain(slot)
        if n_steps >= 2:
            drain(1 - slot)


def _dispatch(pstart, cnt, rstart, rcnt, tbase, eidx, ranks, xn3, n_rows, tm=ROUTE_TILE):
    t = eidx.shape[1]
    meta = pl.BlockSpec((SUBLANES, tm), lambda i, *_: (0, i))
    grid_spec = pltpu.PrefetchScalarGridSpec(
        num_scalar_prefetch=5,
        grid=(t // tm,),
        in_specs=[meta, meta, pl.BlockSpec((tm * SUBLANES, LANES), lambda i, *_: (i, 0))],
        out_specs=pl.BlockSpec(memory_space=pl.ANY),
        scratch_shapes=[pltpu.VMEM((SUBLANES, tm), I32), pltpu.SMEM((SUBLANES, tm), I32),
                        pltpu.SMEM((N_EXPERTS,), I32), pltpu.SMEM((4,), I32),
                        pltpu.VMEM((2 * TOP_K * tm * SUBLANES, LANES), F32),
                        pltpu.VMEM((ROW_BLOCK * SUBLANES, LANES), F32),
                        pltpu.SemaphoreType.DMA, pltpu.SemaphoreType.DMA((2,)), pltpu.SemaphoreType.DMA((2,)),
                        pltpu.SemaphoreType.DMA, pltpu.SemaphoreType.DMA],
    )
    return pl.pallas_call(
        functools.partial(_dispatch_kernel, n_steps=t // tm),
        grid_spec=grid_spec,
        out_shape=jax.ShapeDtypeStruct((n_rows * SUBLANES, LANES), F32),
        compiler_params=_cparams(("arbitrary",)),
        name="dispatch",
    )(pstart, cnt, rstart, rcnt, tbase, eidx, ranks, xn3)


def _experts_kernel(bexp_ref, nblk_ref, nbe_ref, xs_ref, bup_ref, bdn_ref, wup_hbm, wdn_hbm, y_ref,
                    wup32, wdn32, wup16, wdn16, x16, sem_up, sem_dn):
    i = pl.program_id(0)
    rows = x16.shape[0]
    d = x16.shape[1]
    de = wdn16.shape[0]
    e = bexp_ref[i]
    active = i < nblk_ref[0]
    first = (i == 0) | (e != bexp_ref[jnp.maximum(i - 1, 0)])

    def fetch(ex):
        return (pltpu.make_async_copy(wup_hbm.at[ex], wup32, sem_up),
                pltpu.make_async_copy(wdn_hbm.at[ex], wdn32, sem_dn))

    @pl.when(i == 0)
    def _():
        for cp in fetch(e):
            cp.start()

    @pl.when(active & first)
    def _():
        for cp in fetch(e):
            cp.wait()
        wup16[...] = wup32[...].astype(BF16)
        wdn16[...] = wdn32[...].astype(BF16)
        nxt = i + nbe_ref[e]

        @pl.when(nxt < nblk_ref[0])
        def _():
            for cp in fetch(bexp_ref[nxt]):
                cp.start()

    @pl.when(active)
    def _():
        for j in range(d // LANES):
            x16[:, j * LANES:(j + 1) * LANES] = xs_ref[pl.ds(j, rows, stride=SUBLANES), :].astype(BF16)
        hdn = jnp.dot(x16[...], wup16[...], preferred_element_type=F32) + bup_ref[0]
        g = jnp.minimum(hdn[:, :de], SWIGLU_LIMIT)
        u = jnp.clip(hdn[:, de:], -SWIGLU_LIMIT, SWIGLU_LIMIT)
        glu = g * jax.nn.sigmoid(SWIGLU_ALPHA * g)
        act = ((u + 1.0) * glu).astype(BF16)
        y = jnp.dot(act, wdn16[...], preferred_element_type=F32) + bdn_ref[0]
        for j in range(d // LANES):
            y_ref[pl.ds(j, rows, stride=SUBLANES), :] = y[:, j * LANES:(j + 1) * LANES]

    @pl.when(jnp.logical_not(active))
    def _():
        y_ref[...] = jnp.zeros_like(y_ref)


def _experts(bexp, nblk, nbe, xs, w_up, b_up, w_down, b_down):
    n_exp, d, d2 = w_up.shape
    de = w_down.shape[1]
    n_blocks = bexp.shape[0]
    rows = ROW_BLOCK
    xspec = pl.BlockSpec((rows * SUBLANES, LANES), lambda i, be, nb, ne: (i, 0))
    any_spec = pl.BlockSpec(memory_space=pl.ANY)
    grid_spec = pltpu.PrefetchScalarGridSpec(
        num_scalar_prefetch=3,
        grid=(n_blocks,),
        in_specs=[xspec,
                  pl.BlockSpec((1, 1, d2), lambda i, be, nb, ne: (be[i], 0, 0)),
                  pl.BlockSpec((1, 1, d), lambda i, be, nb, ne: (be[i], 0, 0)),
                  any_spec, any_spec],
        out_specs=xspec,
        scratch_shapes=[pltpu.VMEM((d, d2), F32), pltpu.VMEM((de, d), F32),
                        pltpu.VMEM((d, d2), BF16), pltpu.VMEM((de, d), BF16), pltpu.VMEM((rows, d), BF16),
                        pltpu.SemaphoreType.DMA, pltpu.SemaphoreType.DMA],
    )
    return pl.pallas_call(
        _experts_kernel,
        grid_spec=grid_spec,
        out_shape=jax.ShapeDtypeStruct(xs.shape, F32),
        compiler_params=_cparams(("arbitrary",)),
        name="experts",
    )(bexp, nblk, nbe, xs, b_up[:, None, :], b_down[:, None, :], w_up, w_down)


def _combine_kernel(rstart_ref, rcnt_ref, tbase_ref, eidx_ref, rank_ref, gate_ref, h_ref, p_ref, pg_ref, wpg_ref,
                    wp_ref, fg_ref, y_hbm, out_ref,
                    stage, li_vmem, li_smem, gate_smem, delta_smem, nchunk_smem, moe_rows, sem_idx, sem_runs):
    tm = h_ref.shape[0]
    d = h_ref.shape[1]
    i = pl.program_id(0)
    n = pl.num_programs(0)
    slot = lax.rem(i, 2)
    chunk_rows = RUN_CHUNK * SUBLANES
    slot_rows = stage.shape[0] // 2

    def chunks_of(c):
        return (c + (RUN_CHUNK - 1)) >> RUN_SHIFT

    def chunk_copy(src, dst, sl):
        return pltpu.make_async_copy(y_hbm.at[pl.ds(pl.multiple_of(src, SUBLANES), chunk_rows)],
                                     stage.at[pl.ds(pl.multiple_of(dst, SUBLANES), chunk_rows)], sem_runs.at[sl])

    def start_runs(tile, sl):
        def per_expert(ex, dst):
            src = rstart_ref[tile * N_EXPERTS + ex] * SUBLANES
            n_ch = chunks_of(rcnt_ref[tile * N_EXPERTS + ex])

            def one(ch, carry):
                chunk_copy(src + ch * chunk_rows, dst + ch * chunk_rows, sl).start()
                return carry

            lax.fori_loop(0, n_ch, one, 0)
            return dst + n_ch * chunk_rows

        end = lax.fori_loop(0, N_EXPERTS, per_expert, sl * slot_rows)
        nchunk_smem[sl] = (end - sl * slot_rows) >> (RUN_SHIFT + 3)

    @pl.when(i == 0)
    def _():
        start_runs(0, 0)

    @pl.when(i + 1 < n)
    def _():
        start_runs(i + 1, 1 - slot)

    def fill_delta(ex, off):
        delta_smem[ex] = off - tbase_ref[i * N_EXPERTS + ex]
        return off + chunks_of(rcnt_ref[i * N_EXPERTS + ex]) * RUN_CHUNK

    lax.fori_loop(0, N_EXPERTS, fill_delta, 0)
    e = eidx_ref[...]
    delta = jnp.zeros_like(e)
    for ex in range(N_EXPERTS):
        delta = jnp.where(e == ex, delta_smem[ex], delta)
    li_vmem[...] = (rank_ref[...] + delta) * SUBLANES + slot * slot_rows
    to_smem = [pltpu.make_async_copy(li_vmem, li_smem, sem_idx.at[0]),
               pltpu.make_async_copy(gate_ref, gate_smem, sem_idx.at[1])]
    for cp in to_smem:
        cp.start()
    for cp in to_smem:
        cp.wait()

    def wait_chunk(ch, carry):
        chunk_copy(0, slot * slot_rows, slot).wait()
        return carry

    lax.fori_loop(0, nchunk_smem[slot], wait_chunk, 0)

    for tok in range(tm):
        acc = None
        for k in range(TOP_K):
            row = stage[pl.ds(pl.multiple_of(li_smem[k, tok], SUBLANES), SUBLANES), :]
            term = gate_smem[k, tok] * row
            acc = term if acc is None else acc + term
        moe_rows[tok * SUBLANES:(tok + 1) * SUBLANES, :] = acc

    moe = jnp.concatenate([moe_rows[pl.ds(j, tm, stride=SUBLANES), :] for j in range(d // LANES)], axis=1)
    h = h_ref[...] + moe

    ple = jnp.dot(p_ref[...].astype(BF16), wp_ref[...], preferred_element_type=F32)
    ple_gate = jax.nn.sigmoid(jnp.dot(_rms(h, pg_ref[...]).astype(BF16), wpg_ref[...],
                                      preferred_element_type=F32))
    h = h + ple * ple_gate
    out_ref[...] = _rms(h, fg_ref[...])


def _combine(rstart, rcnt, tbase, eidx, ranks, gates, h1, p2, pg, wpg, wp, fg, y, tm=ROUTE_TILE):
    t, d = h1.shape
    dp = p2.shape[1]
    row = lambda i, *_: (i, 0)
    col = lambda i, *_: (0, i)
    const = lambda i, *_: (0, 0)
    meta = pl.BlockSpec((SUBLANES, tm), col)
    assert RUN_CHUNK <= N_EXPERTS
    stage_rows = (TOP_K * tm + N_EXPERTS * RUN_CHUNK) * SUBLANES
    grid_spec = pltpu.PrefetchScalarGridSpec(
        num_scalar_prefetch=3,
        grid=(t // tm,),
        in_specs=[meta, meta, meta, pl.BlockSpec((tm, d), row), pl.BlockSpec((tm, dp), row),
                  pl.BlockSpec((1, d), const), pl.BlockSpec((d, d), const), pl.BlockSpec((dp, d), const),
                  pl.BlockSpec((1, d), const), pl.BlockSpec(memory_space=pl.ANY)],
        out_specs=pl.BlockSpec((tm, d), row),
        scratch_shapes=[pltpu.VMEM((2 * stage_rows, LANES), F32), pltpu.VMEM((SUBLANES, tm), I32),
                        pltpu.SMEM((SUBLANES, tm), I32), pltpu.SMEM((SUBLANES, tm), F32),
                        pltpu.SMEM((N_EXPERTS,), I32), pltpu.SMEM((2,), I32),
                        pltpu.VMEM((tm * SUBLANES, LANES), F32),
                        pltpu.SemaphoreType.DMA((2,)), pltpu.SemaphoreType.DMA((2,))],
    )
    return pl.pallas_call(
        _combine_kernel,
        grid_spec=grid_spec,
        out_shape=jax.ShapeDtypeStruct((t, d), F32),
        compiler_params=_cparams(("arbitrary",)),
        name="combine",
    )(rstart, rcnt, tbase, eidx, ranks, gates, h1, p2, pg, wpg, wp, fg, y)


def _block_diag(w):
    n, d, _ = w.shape
    eye = jnp.eye(n, dtype=w.dtype)
    return (eye[:, None, :, None] * w[:, :, None, :]).reshape(n * d, n * d)


def _stages(x, p, mix_norm_g, w_in, conv_w, conv_b, lru_w_a, lru_b_a, lru_w_x, lru_b_x, lru_lambda,
            lru_out_g, sb_out_g, w_out, ffn_norm_g, w_router, b_router, w_up, b_up, w_down, b_down,
            ple_norm_g, w_ple_gate, w_ple, final_norm_g):
    b, s, d = x.shape
    t = b * s
    st = {}
    x2 = x.reshape(t, d)
    lx, lg, q, k, v = _in_proj(x2, mix_norm_g[0][None], w_in[0].astype(BF16))
    st.update(lru_x=lx, lru_gate=lg, q=q, k=k, v=v)
    row = lambda a: a[None].astype(F32)
    lru_n = _lru(lx.reshape(b, s, D_LRU), lg.reshape(b, s, D_LRU), conv_w[0], row(conv_b[0]),
                 _block_diag(lru_w_a[0]).astype(BF16), row(lru_b_a[0]),
                 _block_diag(lru_w_x[0]).astype(BF16), row(lru_b_x[0]),
                 row(lru_lambda[0]), row(lru_out_g[0]))
    st["lru_n"] = lru_n
    sb_y = _sb_attn(q.reshape(b, s, D_SB), k.reshape(b, s, D_SB), v.reshape(b, s, D_SB))
    st["sb_y"] = sb_y
    wo = w_out[0].astype(BF16)
    h1, xn3, eidx, gates, ranks, counts, tile_base = _out_route(
        x2, lru_n.reshape(t, D_LRU), sb_y.reshape(t, D_SB), row(sb_out_g[0]), wo[:D_LRU], wo[D_LRU:],
        row(ffn_norm_g[0]), w_router[0].T, b_router[0][:, None])
    st.update(h1=h1, xn1=xn3, eidx=eidx, gates=gates, ranks=ranks, counts=counts)

    cnt = counts[:, 0].astype(I32)
    padded = (cnt + ROW_BLOCK - 1) // ROW_BLOCK * ROW_BLOCK
    pend = jnp.cumsum(padded)
    pstart = pend - padded
    n_blocks = -(-(t * TOP_K) // ROW_BLOCK) + N_EXPERTS
    block_row0 = jnp.arange(n_blocks, dtype=I32) * ROW_BLOCK
    bexp = jnp.minimum(jnp.sum((pend[None, :] <= block_row0[:, None]).astype(I32), axis=1), N_EXPERTS - 1)
    nblk = (pend[-1:] // ROW_BLOCK).astype(I32)

    tbase = tile_base[:, 0].astype(I32).reshape(-1, N_EXPERTS)
    rcnt = jnp.concatenate([tbase[1:], cnt[None, :]], axis=0) - tbase
    rstart = pstart[None, :] + tbase

    xs = _dispatch(pstart, cnt, rstart.reshape(-1), rcnt.reshape(-1), tbase.reshape(-1), eidx, ranks, xn3,
                   n_blocks * ROW_BLOCK)
    y = _experts(bexp, nblk, padded // ROW_BLOCK, xs, w_up[0], b_up[0], w_down[0], b_down[0])
    st.update(xs=xs, y=y)
    out = _combine(rstart.reshape(-1), rcnt.reshape(-1), tbase.reshape(-1), eidx, ranks, gates, h1,
                   p[0].reshape(t, -1), row(ple_norm_g[0]), w_ple_gate[0].astype(BF16),
                   w_ple[0].astype(BF16), row(final_norm_g), y)
    st["final"] = out.reshape(b, s, d)
    return st


def kernel(x, p, mix_norm_g, w_in, conv_w, conv_b, lru_w_a, lru_b_a, lru_w_x, lru_b_x, lru_lambda,
           lru_out_g, sb_out_g, w_out, ffn_norm_g, w_router, b_router, w_up, b_up, w_down, b_down,
           ple_norm_g, w_ple_gate, w_ple, final_norm_g):
    return _stages(x, p, mix_norm_g, w_in, conv_w, conv_b, lru_w_a, lru_b_a, lru_w_x, lru_b_x, lru_lambda,
                   lru_out_g, sb_out_g, w_out, ffn_norm_g, w_router, b_router, w_up, b_up, w_down, b_down,
                   ple_norm_g, w_ple_gate, w_ple, final_norm_g)["final"]
```
